```python
import math
import jax, jax.numpy as jnp
from jax import lax
import numpy as np

D_MODEL = 1024
BATCH = 4
SEQ = 4096
DEPTH = 4
DEC_BATCH = 128
DEC_SEQ = 8
PAST_LEN = 2048
PAGE_SIZE = 128

HEAD_DIM = 64
DIFF_HEADS = (D_MODEL // 2) // (2 * HEAD_DIM)
MOBA_HEADS = (D_MODEL // 2) // HEAD_DIM
SB_HEADS = D_MODEL // HEAD_DIM
KV_WIDTH = D_MODEL
D_FF = ((8 * D_MODEL // 3 + 127) // 128) * 128
MOBA_BLOCK = 256
MOBA_TOPK = 3
ROPE_THETA = 10000.0
NORM_EPS = 1e-6
Q_BLOCK = 128
MOBA_Q_BLOCK = 32
N_EVEN = (DEPTH + 1) // 2
N_NORMS = 6

kernel_name = 'hybrid_diff_moba_stickbreak_decoder_step'

F32 = jnp.float32


def rms_norm(x, g):
    xf = x.astype(F32)
    y = xf * lax.rsqrt(jnp.mean(xf * xf, axis=-1, keepdims=True) + NORM_EPS)
    return (y * g.astype(F32)).astype(x.dtype)


def swiglu(x, w_in, w_out):
    gate, up = jnp.split(x @ w_in, 2, axis=-1)
    return (jax.nn.silu(gate) * up) @ w_out


def rope(x, pos):
    d = x.shape[-1]
    half = d // 2
    inv_freq = 1.0 / (ROPE_THETA ** (jnp.arange(half, dtype=F32) * (2.0 / d)))
    ang = pos.astype(F32)[:, None] * inv_freq[None, :]
    cos = jnp.cos(ang)[None, :, None, :]
    sin = jnp.sin(ang)[None, :, None, :]
    xf = x.astype(F32)
    x1, x2 = xf[..., :half], xf[..., half:]
    return jnp.concatenate([x1 * cos - x2 * sin, x2 * cos + x1 * sin], axis=-1).astype(x.dtype)


def lambda_init(layer):
    return 0.8 - 0.6 * math.exp(-0.3 * layer)


def sweep_queries(fn, q, q_pos, block):
    B, T = q.shape[0], q.shape[1]
    nb = T // block
    qb = q.reshape((B, nb, block) + q.shape[2:]).swapaxes(0, 1)
    pb = q_pos.reshape(nb, block)
    out = lax.map(lambda a: fn(a[0], a[1]), (qb, pb))
    out = out.swapaxes(0, 1)
    return out.reshape((B, T) + out.shape[3:])


def diff_attention(q, q_pos, k, v, lam, block):
    k_pos = jnp.arange(k.shape[1], dtype=jnp.int32)
    k1, k2 = k[..., :HEAD_DIM], k[..., HEAD_DIM:]
    vf = v.astype(F32)
    scale = HEAD_DIM ** -0.5

    def fn(qc, pc):
        mask = k_pos[None, :] <= pc[:, None]

        def probs(qh, kh):
            s = jnp.einsum('bthd,bshd->bhts', qh, kh).astype(F32) * scale
            return jax.nn.softmax(jnp.where(mask, s, -jnp.inf), axis=-1)

        a = probs(qc[..., :HEAD_DIM], k1) - lam * probs(qc[..., HEAD_DIM:], k2)
        return jnp.einsum('bhts,bshd->bthd', a, vf).astype(qc.dtype)

    return sweep_queries(fn, q, q_pos, block)


def moba_attention(q, q_pos, k, v, block):
    B, L, H, d = k.shape
    nb = max(-(-L // MOBA_BLOCK), MOBA_TOPK)
    pad = nb * MOBA_BLOCK - L
    kb = jnp.pad(k, ((0, 0), (0, pad), (0, 0), (0, 0))).reshape(B, nb, MOBA_BLOCK, H, d).transpose(0, 3, 1, 2, 4)
    vb = jnp.pad(v, ((0, 0), (0, pad), (0, 0), (0, 0))).reshape(B, nb, MOBA_BLOCK, H, d).transpose(0, 3, 1, 2, 4)
    kbar = jnp.mean(kb.astype(F32), axis=3)
    bi = jnp.arange(B)[:, None, None, None]
    hi = jnp.arange(H)[None, :, None, None]
    blk_ids = jnp.arange(nb, dtype=jnp.int32)
    in_blk = jnp.arange(MOBA_BLOCK, dtype=jnp.int32)
    scale = d ** -0.5

    def fn(qc, pc):
        c = qc.shape[1]
        qblk = pc // MOBA_BLOCK
        gate = jnp.einsum('bthd,bhnd->bhtn', qc.astype(F32), kbar)
        eligible = blk_ids[None, :] < qblk[:, None]
        gate = jnp.where(eligible, gate, -jnp.inf)
        _, idx = lax.top_k(gate, MOBA_TOPK)
        sel_ok = idx < qblk[None, None, :, None]
        k_sel = kb[bi, hi, idx]
        v_sel = vb[bi, hi, idx]
        own = qblk[0]
        k_own = lax.dynamic_index_in_dim(kb, own, axis=2, keepdims=False)
        v_own = lax.dynamic_index_in_dim(vb, own, axis=2, keepdims=False)
        own_ok = (own * MOBA_BLOCK + in_blk)[None, :] <= pc[:, None]
        s_sel = jnp.einsum('bthd,bhtjsd->bhtjs', qc, k_sel).astype(F32) * scale
        s_sel = jnp.where(sel_ok[..., None], s_sel, -jnp.inf).reshape(B, H, c, MOBA_TOPK * MOBA_BLOCK)
        s_own = jnp.einsum('bthd,bhsd->bhts', qc, k_own).astype(F32) * scale
        s_own = jnp.where(own_ok, s_own, -jnp.inf)
        p = jax.nn.softmax(jnp.concatenate([s_sel, s_own], axis=-1), axis=-1)
        p_sel = p[..., :MOBA_TOPK * MOBA_BLOCK].reshape(B, H, c, MOBA_TOPK, MOBA_BLOCK)
        p_own = p[..., MOBA_TOPK * MOBA_BLOCK:]
        out = (jnp.einsum('bhtjs,bhtjsd->bthd', p_sel, v_sel.astype(F32))
               + jnp.einsum('bhts,bhsd->bthd', p_own, v_own.astype(F32)))
        return out.astype(qc.dtype)

    return sweep_queries(fn, q, q_pos, block)


def stick_breaking_attention(q, q_pos, k, v, block):
    k_pos = jnp.arange(k.shape[1], dtype=jnp.int32)
    vf = v.astype(F32)
    scale = HEAD_DIM ** -0.5

    def fn(qc, pc):
        z = jnp.einsum('bthd,bshd->bhts', qc, k).astype(F32) * scale
        strict = k_pos[None, :] < pc[:, None]
        log_keep = jnp.where(strict, jax.nn.log_sigmoid(-z), 0.0)
        suffix = lax.cumsum(log_keep, axis=3, reverse=True) - log_keep
        a = jnp.where(strict, jnp.exp(jax.nn.log_sigmoid(z) + suffix), 0.0)
        return jnp.einsum('bhts,bshd->bthd', a, vf).astype(qc.dtype)

    return sweep_queries(fn, q, q_pos, block)


def append_past(past, new):
    return new if past is None else jnp.concatenate([past, new], axis=1)


def diff_moba_mixer(xn, q_pos, past_k, past_v, w_in, w_out, lam_vecs, subln_g, lam_init, q_block, moba_q_block):
    B, T, _ = xn.shape
    half = D_MODEL // 2
    qa, ka, va, qb, kb, vb = jnp.split(xn @ w_in, 6, axis=-1)
    qa = rope(qa.reshape(B, T, 2 * DIFF_HEADS, HEAD_DIM), q_pos).reshape(B, T, DIFF_HEADS, 2 * HEAD_DIM)
    ka = rope(ka.reshape(B, T, 2 * DIFF_HEADS, HEAD_DIM), q_pos).reshape(B, T, DIFF_HEADS, 2 * HEAD_DIM)
    qb = rope(qb.reshape(B, T, MOBA_HEADS, HEAD_DIM), q_pos)
    kb = rope(kb.reshape(B, T, MOBA_HEADS, HEAD_DIM), q_pos)
    k_new = jnp.concatenate([ka.reshape(B, T, half), kb.reshape(B, T, half)], axis=-1)
    v_new = jnp.concatenate([va, vb], axis=-1)
    k_all = append_past(past_k, k_new)
    v_all = append_past(past_v, v_new)
    L = k_all.shape[1]
    lv = lam_vecs.astype(F32)
    lam = jnp.exp(jnp.sum(lv[0] * lv[1])) - jnp.exp(jnp.sum(lv[2] * lv[3])) + lam_init
    o_a = diff_attention(qa, q_pos,
                         k_all[..., :half].reshape(B, L, DIFF_HEADS, 2 * HEAD_DIM),
                         v_all[..., :half].reshape(B, L, DIFF_HEADS, 2 * HEAD_DIM), lam, q_block)
    o_a = rms_norm(o_a, subln_g) * (1.0 - lam_init)
    o_b = moba_attention(qb, q_pos,
                         k_all[..., half:].reshape(B, L, MOBA_HEADS, HEAD_DIM),
                         v_all[..., half:].reshape(B, L, MOBA_HEADS, HEAD_DIM), moba_q_block)
    o = jnp.concatenate([o_a.reshape(B, T, half), o_b.reshape(B, T, half)], axis=-1)
    return o @ w_out, k_new, v_new


def stick_breaking_mixer(xn, q_pos, past_k, past_v, w_in, w_out, q_block):
    B, T, _ = xn.shape
    q, k_new, v_new = jnp.split(xn @ w_in, 3, axis=-1)
    k_all = append_past(past_k, k_new)
    v_all = append_past(past_v, v_new)
    L = k_all.shape[1]
    o = stick_breaking_attention(q.reshape(B, T, SB_HEADS, HEAD_DIM), q_pos,
                                 k_all.reshape(B, L, SB_HEADS, HEAD_DIM),
                                 v_all.reshape(B, L, SB_HEADS, HEAD_DIM), q_block)
    return o.reshape(B, T, D_MODEL) @ w_out, k_new, v_new


def run_trunk(x, q_pos, cache_k, cache_v, page_table, norm_gains, ffn1_w_in, ffn1_w_out, mix_w_in, mix_w_out,
              diff_lambda, diff_subln, ffn2_w_in, ffn2_w_out, q_block, moba_q_block):
    B = x.shape[0]
    new_k, new_v = [], []
    for l in range(DEPTH):
        g = norm_gains[l]
        x = x + 0.5 * rms_norm(swiglu(rms_norm(x, g[0]), ffn1_w_in[l], ffn1_w_out[l]), g[1])
        if cache_k is None:
            past_k = past_v = None
        else:
            past_k = cache_k[l, page_table].reshape(B, -1, KV_WIDTH)
            past_v = cache_v[l, page_table].reshape(B, -1, KV_WIDTH)
        xn = rms_norm(x, g[2])
        if l % 2 == 0:
            j = l // 2
            m, k_l, v_l = diff_moba_mixer(xn, q_pos, past_k, past_v, mix_w_in[l], mix_w_out[l], diff_lambda[j],
                                          diff_subln[j], lambda_init(l), q_block, moba_q_block)
        else:
            m, k_l, v_l = stick_breaking_mixer(xn, q_pos, past_k, past_v, mix_w_in[l], mix_w_out[l], q_block)
        x = x + rms_norm(m, g[3])
        x = x + 0.5 * rms_norm(swiglu(rms_norm(x, g[4]), ffn2_w_in[l], ffn2_w_out[l]), g[5])
        new_k.append(k_l)
        new_v.append(v_l)
    return x, jnp.stack(new_k), jnp.stack(new_v)


def setup_inputs(seed: int = 0) -> dict:
    key = jax.random.key(seed)
    ks = jax.random.split(key, 16)
    n_pages = PAST_LEN // PAGE_SIZE
    n_used = DEC_BATCH * n_pages
    n_pool = n_used + n_used // 4
    nrm = jax.random.normal
    x_prompt = nrm(ks[0], (BATCH, SEQ, D_MODEL), F32)
    x_sample = nrm(ks[1], (DEC_BATCH, DEC_SEQ, D_MODEL), F32)
    cache_k = nrm(ks[2], (DEPTH, n_pool, PAGE_SIZE, KV_WIDTH), F32)
    cache_v = nrm(ks[3], (DEPTH, n_pool, PAGE_SIZE, KV_WIDTH), F32)
    page_table = jax.random.permutation(ks[4], n_pool)[:n_used].reshape(DEC_BATCH, n_pages).astype(jnp.int32)
    norm_gains = 1.0 + 0.05 * nrm(ks[5], (DEPTH, N_NORMS, D_MODEL), F32)
    ffn1_w_in = nrm(ks[6], (DEPTH, D_MODEL, 2 * D_FF), F32) * D_MODEL ** -0.5
    ffn1_w_out = nrm(ks[7], (DEPTH, D_FF, D_MODEL), F32) * D_FF ** -0.5
    mix_w_in = nrm(ks[8], (DEPTH, D_MODEL, 3 * D_MODEL), F32) * D_MODEL ** -0.5
    mix_w_out = nrm(ks[9], (DEPTH, D_MODEL, D_MODEL), F32) * D_MODEL ** -0.5
    diff_lambda = 0.1 * nrm(ks[10], (N_EVEN, 4, HEAD_DIM), F32)
    diff_subln = 1.0 + 0.05 * nrm(ks[11], (N_EVEN, 2 * HEAD_DIM), F32)
    ffn2_w_in = nrm(ks[12], (DEPTH, D_MODEL, 2 * D_FF), F32) * D_MODEL ** -0.5
    ffn2_w_out = nrm(ks[13], (DEPTH, D_FF, D_MODEL), F32) * D_FF ** -0.5
    return {'x_prompt': x_prompt, 'x_sample': x_sample, 'cache_k': cache_k, 'cache_v': cache_v,
            'page_table': page_table, 'norm_gains': norm_gains, 'ffn1_w_in': ffn1_w_in, 'ffn1_w_out': ffn1_w_out,
            'mix_w_in': mix_w_in, 'mix_w_out': mix_w_out, 'diff_lambda': diff_lambda, 'diff_subln': diff_subln,
            'ffn2_w_in': ffn2_w_in, 'ffn2_w_out': ffn2_w_out}


def reference(x_prompt, x_sample, cache_k, cache_v, page_table, norm_gains, ffn1_w_in, ffn1_w_out, mix_w_in,
              mix_w_out, diff_lambda, diff_subln, ffn2_w_in, ffn2_w_out):
    pos_prompt = jnp.arange(SEQ, dtype=jnp.int32)
    y_prompt, new_k_prompt, new_v_prompt = run_trunk(
        x_prompt, pos_prompt, None, None, None, norm_gains, ffn1_w_in, ffn1_w_out, mix_w_in, mix_w_out,
        diff_lambda, diff_subln, ffn2_w_in, ffn2_w_out, Q_BLOCK, MOBA_Q_BLOCK)
    pos_sample = PAST_LEN + jnp.arange(DEC_SEQ, dtype=jnp.int32)
    y_sample, new_k_sample, new_v_sample = run_trunk(
        x_sample, pos_sample, cache_k, cache_v, page_table, norm_gains, ffn1_w_in, ffn1_w_out, mix_w_in, mix_w_out,
        diff_lambda, diff_subln, ffn2_w_in, ffn2_w_out, DEC_SEQ, 1)
    return (y_prompt, y_sample, new_k_prompt, new_v_prompt, new_k_sample, new_v_sample)
```

```python
import functools
import math

import jax
import jax.numpy as jnp
from jax import lax
from jax.experimental import pallas as pl
from jax.experimental.pallas import tpu as pltpu

F32 = jnp.float32
BF16 = jnp.bfloat16

D_MODEL = 1024
HEAD_DIM = 64
D_FF = 2816
DEPTH = 4
PAGE_SIZE = 128
PAST_LEN = 2048
N_PAGES = PAST_LEN // PAGE_SIZE
MOBA_BLOCK = 256
MOBA_TOPK = 3
ROPE_THETA = 10000.0
NORM_EPS = 1e-6
SCALE = HEAD_DIM ** -0.5
HALF = D_MODEL // 2
LANES = 128
NEG = -1e30

VMEM_LIMIT = 56 * 1024 * 1024

FFN_TM = 512
FFN_TF = 1408
ATT_TQ = 256
SB_TK = 128


def _params(sem):
    return pltpu.CompilerParams(dimension_semantics=sem, vmem_limit_bytes=VMEM_LIMIT)


def _rms(x, g):
    return x * lax.rsqrt(jnp.mean(x * x, axis=-1, keepdims=True) + NORM_EPS) * g


def _dot_nt(a, b):
    return lax.dot_general(a, b, (((1,), (1,)), ((), ())), preferred_element_type=F32)


def _dot(a, b):
    return jnp.dot(a, b, preferred_element_type=F32)


def _iota(shape, dim):
    return lax.broadcasted_iota(jnp.int32, shape, dim)


def _ffn_kernel(x_ref, gpre_ref, gpost_ref, wg_ref, wu_ref, wo_ref, o_ref, xn_ref, acc_ref):
    j = pl.program_id(1)

    @pl.when(j == 0)
    def _():
        xn_ref[...] = _rms(x_ref[...], gpre_ref[...]).astype(BF16)

    xn = xn_ref[...]
    gate = _dot(xn, wg_ref[...])
    up = _dot(xn, wu_ref[...])
    h = (gate * jax.nn.sigmoid(gate) * up).astype(BF16)
    part = _dot(h, wo_ref[...])

    @pl.when(j == 0)
    def _():
        acc_ref[...] = part

    @pl.when(j > 0)
    def _():
        acc_ref[...] += part

    @pl.when(j == pl.num_programs(1) - 1)
    def _():
        o_ref[...] = x_ref[...] + 0.5 * _rms(acc_ref[...], gpost_ref[...])


def _ffn(x, g_pre, g_post, w_in, w_out, layer):
    n = x.shape[0]
    nf = D_FF // FFN_TF
    return pl.pallas_call(
        _ffn_kernel,
        grid=(n // FFN_TM, nf),
        in_specs=[
            pl.BlockSpec((FFN_TM, D_MODEL), lambda i, j: (i, 0)),
            pl.BlockSpec((1, D_MODEL), lambda i, j: (0, 0)),
            pl.BlockSpec((1, D_MODEL), lambda i, j: (0, 0)),
            pl.BlockSpec((None, D_MODEL, FFN_TF), lambda i, j: (layer, 0, j)),
            pl.BlockSpec((None, D_MODEL, FFN_TF), lambda i, j: (layer, 0, j + nf)),
            pl.BlockSpec((None, FFN_TF, D_MODEL), lambda i, j: (layer, j, 0)),
        ],
        out_specs=pl.BlockSpec((FFN_TM, D_MODEL), lambda i, j: (i, 0)),
        out_shape=jax.ShapeDtypeStruct((n, D_MODEL), F32),
        scratch_shapes=[pltpu.VMEM((FFN_TM, D_MODEL), BF16), pltpu.VMEM((FFN_TM, D_MODEL), F32)],
        compiler_params=_params(("parallel", "arbitrary")),
    )(x, g_pre, g_post, w_in, w_in, w_out)


def _inproj_kernel(x_ref, g_ref, w_ref, cos_ref, sina_ref, sinb_ref, q_ref, k_ref, v_ref, *, rotary):
    xn = _rms(x_ref[...], g_ref[...]).astype(BF16)
    outs = (q_ref, k_ref, v_ref)
    for c in range(3):
        y = _dot(xn, w_ref[:, c * D_MODEL:(c + 1) * D_MODEL])
        if rotary and c < 2:
            cos, sina, sinb = cos_ref[...], sina_ref[...], sinb_ref[...]
            for p in range(D_MODEL // LANES):
                yp = y[:, p * LANES:(p + 1) * LANES]
                fwd = pltpu.roll(yp, LANES - HEAD_DIM // 2, 1)
                bwd = pltpu.roll(yp, HEAD_DIM // 2, 1)
                outs[c][:, p * LANES:(p + 1) * LANES] = yp * cos + fwd * sina + bwd * sinb
        else:
            outs[c][...] = y


def _inproj(x, g, w, layer, tables, rotary):
    n = x.shape[0]
    cos, sina, sinb = tables
    nt = cos.shape[0] // FFN_TM
    tab = pl.BlockSpec((FFN_TM, LANES), lambda i: (i % nt, 0))
    row = pl.BlockSpec((FFN_TM, D_MODEL), lambda i: (i, 0))
    return pl.pallas_call(
        functools.partial(_inproj_kernel, rotary=rotary),
        grid=(n // FFN_TM,),
        in_specs=[
            row,
            pl.BlockSpec((1, D_MODEL), lambda i: (0, 0)),
            pl.BlockSpec((None, D_MODEL, 3 * D_MODEL), lambda i: (layer, 0, 0)),
            tab, tab, tab,
        ],
        out_specs=[row, row, row],
        out_shape=[jax.ShapeDtypeStruct((n, D_MODEL), F32)] * 3,
        compiler_params=_params(("parallel",)),
    )(x, g, w, cos, sina, sinb)


def _outproj_kernel(o_ref, x_ref, w_ref, g_ref, y_ref):
    m = _dot(o_ref[...], w_ref[...])
    y_ref[...] = x_ref[...] + _rms(m, g_ref[...])


def _outproj(o, x, w, g, layer):
    n = x.shape[0]
    row = pl.BlockSpec((FFN_TM, D_MODEL), lambda i: (i, 0))
    return pl.pallas_call(
        _outproj_kernel,
        grid=(n // FFN_TM,),
        in_specs=[
            row, row,
            pl.BlockSpec((None, D_MODEL, D_MODEL), lambda i: (layer, 0, 0)),
            pl.BlockSpec((1, D_MODEL), lambda i: (0, 0)),
        ],
        out_specs=row,
        out_shape=jax.ShapeDtypeStruct((n, D_MODEL), F32),
        compiler_params=_params(("parallel",)),
    )(o, x, w, g)


def _stack_heads(q):
    low = _iota(q.shape, 1) < HEAD_DIM
    return jnp.concatenate([jnp.where(low, q, 0.0), jnp.where(low, 0.0, q)], axis=0)


def _softmax_step(s, vb, m_ref, l_ref, acc_ref):
    m_old = m_ref[...]
    m_new = jnp.maximum(m_old, jnp.max(s, axis=1, keepdims=True))
    alpha = jnp.exp(m_old - m_new)
    p = jnp.exp(s - m_new)
    l_ref[...] = alpha * l_ref[...] + jnp.sum(p, axis=1, keepdims=True)
    acc_ref[...] = alpha * acc_ref[...] + _dot(p.astype(BF16), vb)
    m_ref[...] = m_new


def _lambda_value(lam_ref, lam_init):
    lv = lam_ref[...]
    a = jnp.sum(lv[0:1] * lv[1:2], axis=1, keepdims=True)
    b = jnp.sum(lv[2:3] * lv[3:4], axis=1, keepdims=True)
    return jnp.exp(a) - jnp.exp(b) + lam_init


def _sb_step(z, strict, vb, carry_ref, acc_ref):
    tk = z.shape[1]
    lp = jnp.log1p(jnp.exp(-jnp.abs(z)))
    log_beta = jnp.minimum(z, 0.0) - lp
    log_keep = -jnp.maximum(z, 0.0) - lp
    if strict is not None:
        log_keep = jnp.where(strict, log_keep, 0.0)
    later = (_iota((tk, tk), 0) > _iota((tk, tk), 1)).astype(BF16)
    hi = log_keep.astype(BF16)
    lo = (log_keep - hi.astype(F32)).astype(BF16)
    suffix = _dot(hi, later) + _dot(lo, later)
    carry = carry_ref[...]
    a = jnp.exp(log_beta + suffix + carry)
    if strict is not None:
        a = jnp.where(strict, a, 0.0)
    acc_ref[...] += _dot(a.astype(BF16), vb)
    carry_ref[...] = carry + jnp.sum(log_keep, axis=1, keepdims=True)


def _rank_select(gate, n_blocks, eligible):
    lane = _iota(gate.shape, 1)
    g = jnp.where(eligible, gate, -jnp.inf)
    cnt = jnp.zeros(gate.shape, jnp.int32)
    for jp in range(n_blocks):
        col = g[:, jp:jp + 1]
        beats = (col > g) | ((col == g) & (jp < lane))
        cnt = cnt + jnp.where(beats, 1, 0)
    return (cnt < MOBA_TOPK) & eligible


def _diff_prompt_kernel(lam_ref, sg_ref, q_ref, k_ref, v_ref, o_ref, m_ref, l_ref, acc_ref, *, lam_init):
    i = pl.program_id(2)
    tq = ATT_TQ
    qs = _stack_heads(q_ref[0] * SCALE).astype(BF16)
    m_ref[...] = jnp.full(m_ref.shape, NEG, F32)
    l_ref[...] = jnp.zeros(l_ref.shape, F32)
    acc_ref[...] = jnp.zeros(acc_ref.shape, F32)

    def block(j, masked):
        kb = k_ref[0, pl.ds(j * tq, tq), :].astype(BF16)
        vb = v_ref[0, pl.ds(j * tq, tq), :].astype(BF16)
        s = _dot_nt(qs, kb)
        if masked:
            qpos = _iota(s.shape, 0) & (tq - 1)
            s = jnp.where(_iota(s.shape, 1) <= qpos, s, NEG)
        _softmax_step(s, vb, m_ref, l_ref, acc_ref)

    def body(j, c):
        block(j, False)
        return c

    lax.fori_loop(0, i, body, 0)
    block(i, True)

    o = acc_ref[...] / l_ref[...]
    a = o[:tq] - _lambda_value(lam_ref, lam_init) * o[tq:]
    o_ref[0] = (_rms(a, sg_ref[...]) * (1.0 - lam_init)).astype(o_ref.dtype)


def _moba_prompt_kernel(q_ref, k_ref, v_ref, o_ref, kbar_ref, m_ref, l_ref, acc_ref):
    i = pl.program_id(2)
    tq = ATT_TQ
    n_blocks = k_ref.shape[1] // MOBA_BLOCK

    @pl.when(i == 0)
    def _():
        kbar_ref[...] = jnp.zeros(kbar_ref.shape, F32)
        for jb in range(n_blocks):
            blk = k_ref[0, jb * MOBA_BLOCK:(jb + 1) * MOBA_BLOCK, :]
            kbar_ref[jb:jb + 1, :] = jnp.sum(blk, axis=0, keepdims=True) * (1.0 / MOBA_BLOCK)

    qf = _stack_heads(q_ref[0])
    gate = lax.dot_general(qf, kbar_ref[...], (((1,), (1,)), ((), ())),
                           precision=lax.Precision.HIGHEST, preferred_element_type=F32)
    sel = _rank_select(gate, n_blocks, _iota(gate.shape, 1) < i)
    bias = jnp.where(sel, 0.0, NEG).astype(BF16)
    qs = (qf * SCALE).astype(BF16)
    qa = jnp.concatenate([qs, bias], axis=1)

    m_ref[...] = jnp.full(m_ref.shape, NEG, F32)
    l_ref[...] = jnp.zeros(l_ref.shape, F32)
    acc_ref[...] = jnp.zeros(acc_ref.shape, F32)

    def body(j, c):
        kb = k_ref[0, pl.ds(j * tq, tq), :].astype(BF16)
        vb = v_ref[0, pl.ds(j * tq, tq), :].astype(BF16)
        onehot = jnp.where(_iota(kb.shape, 1) == j, 1.0, 0.0).astype(BF16)
        s = _dot_nt(qa, jnp.concatenate([kb, onehot], axis=1))
        _softmax_step(s, vb, m_ref, l_ref, acc_ref)
        return c

    lax.fori_loop(0, i, body, 0)

    kb = k_ref[0, pl.ds(i * tq, tq), :].astype(BF16)
    vb = v_ref[0, pl.ds(i * tq, tq), :].astype(BF16)
    s = _dot_nt(qs, kb)
    qpos = _iota(s.shape, 0) & (tq - 1)
    s = jnp.where(_iota(s.shape, 1) <= qpos, s, NEG)
    _softmax_step(s, vb, m_ref, l_ref, acc_ref)

    o = acc_ref[...] / l_ref[...]
    low = _iota((tq, LANES), 1) < HEAD_DIM
    o_ref[0] = jnp.where(low, o[:tq], o[tq:]).astype(o_ref.dtype)


def _sb_prompt_kernel(q_ref, k_ref, v_ref, o_ref, carry_ref, acc_ref):
    i = pl.program_id(2)
    tq, tk = ATT_TQ, SB_TK
    qs = _stack_heads(q_ref[0] * SCALE).astype(BF16)
    carry_ref[...] = jnp.zeros(carry_ref.shape, F32)
    acc_ref[...] = jnp.zeros(acc_ref.shape, F32)
    n_kb = (i + 1) * (tq // tk)

    def body(n, c):
        j = n_kb - 1 - n
        kb = k_ref[0, pl.ds(j * tk, tk), :].astype(BF16)
        vb = v_ref[0, pl.ds(j * tk, tk), :].astype(BF16)
        z = _dot_nt(qs, kb)
        qpos = i * tq + (_iota(z.shape, 0) & (tq - 1))
        strict = (j * tk + _iota(z.shape, 1)) < qpos
        _sb_step(z, strict, vb, carry_ref, acc_ref)
        return c

    lax.fori_loop(0, n_kb, body, 0)
    acc = acc_ref[...]
    low = _iota((tq, LANES), 1) < HEAD_DIM
    o_ref[0] = jnp.where(low, acc[:tq], acc[tq:]).astype(o_ref.dtype)


def _prompt_attention_call(kernel, q, k, v, col0, n_col, extra_in, extra_specs, scratch):
    b, t, _ = q.shape
    tq = ATT_TQ
    qspec = pl.BlockSpec((1, tq, LANES), lambda bi, c, i: (bi, i, col0 + c))
    kvspec = pl.BlockSpec((1, t, LANES), lambda bi, c, i: (bi, 0, col0 + c))
    return pl.pallas_call(
        kernel,
        grid=(b, n_col, t // tq),
        in_specs=list(extra_specs) + [qspec, kvspec, kvspec],
        out_specs=pl.BlockSpec((1, tq, LANES), lambda bi, c, i: (bi, i, c)),
        out_shape=jax.ShapeDtypeStruct((b, t, n_col * LANES), BF16),
        scratch_shapes=scratch,
        compiler_params=_params(("parallel", "parallel", "arbitrary")),
    )(*extra_in, q, k, v)


def _stat_scratch(rows):
    return [pltpu.VMEM((rows, 1), F32), pltpu.VMEM((rows, 1), F32), pltpu.VMEM((rows, LANES), F32)]


def _prompt_even_attention(q, k, v, lam_vecs, subln, lam_init):
    whole = lambda shape: pl.BlockSpec(shape, lambda bi, c, i: (0, 0))
    o_a = _prompt_attention_call(
        functools.partial(_diff_prompt_kernel, lam_init=lam_init), q, k, v, 0, HALF // LANES,
        (lam_vecs, subln), (whole(lam_vecs.shape), whole(subln.shape)), _stat_scratch(2 * ATT_TQ))
    o_b = _prompt_attention_call(
        _moba_prompt_kernel, q, k, v, HALF // LANES, HALF // LANES, (), (),
        [pltpu.VMEM((LANES, LANES), F32)] + _stat_scratch(2 * ATT_TQ))
    return jnp.concatenate([o_a, o_b], axis=-1)


def _prompt_odd_attention(q, k, v):
    return _prompt_attention_call(
        _sb_prompt_kernel, q, k, v, 0, D_MODEL // LANES, (), (),
        [pltpu.VMEM((2 * ATT_TQ, 1), F32), pltpu.VMEM((2 * ATT_TQ, LANES), F32)])


def _block_diag(q, n_heads, width):
    t = q.shape[0]
    tiled = jnp.concatenate([q] * n_heads, axis=0)
    keep = _head_mask(tiled.shape, t, width)
    return jnp.where(keep, tiled, 0.0)


def _head_mask(shape, rows_per_head, width):
    shift_r = rows_per_head.bit_length() - 1
    shift_c = width.bit_length() - 1
    return (_iota(shape, 0) >> shift_r) == (_iota(shape, 1) >> shift_c)


def _fold_heads(x, t):
    out = x[0:t]
    for h in range(1, x.shape[0] // t):
        out = out + x[h * t:(h + 1) * t]
    return out


def _pad_rows(x, rows):
    return jnp.concatenate([x, jnp.zeros((rows - x.shape[0], x.shape[1]), x.dtype)], axis=0)


def _even_decode_kernel(pt_ref, lam_ref, sg_ref, q_ref, kn_ref, vn_ref, ck_ref, cv_ref, o_ref,
                        qd_ref, qm_ref, qmf_ref, md_ref, ld_ref, accd_ref,
                        mb_ref, lb_ref, accb_ref, ks_ref, gate_ref, *, lam_init, t_new):
    del pt_ref
    p = pl.program_id(1)
    n_maps = HALF // HEAD_DIM
    rows = n_maps * t_new
    pages_per_block = MOBA_BLOCK // PAGE_SIZE
    n_blocks = N_PAGES // pages_per_block

    @pl.when(p == 0)
    def _():
        q = q_ref[0]
        qd_ref[...] = (_block_diag(q[:, :HALF], n_maps, HEAD_DIM) * SCALE).astype(BF16)
        qmf = _block_diag(q[:, HALF:], n_maps, HEAD_DIM)
        qmf_ref[...] = qmf
        qm_ref[...] = (qmf * SCALE).astype(BF16)
        md_ref[...] = jnp.full(md_ref.shape, NEG, F32)
        ld_ref[...] = jnp.zeros(ld_ref.shape, F32)
        accd_ref[...] = jnp.zeros(accd_ref.shape, F32)
        gate_ref[...] = jnp.zeros(gate_ref.shape, F32)

    @pl.when(p < N_PAGES)
    def _():
        kpage = ck_ref[...]
        vpage = cv_ref[...]
        s = _dot_nt(qd_ref[...], kpage[:, :HALF].astype(BF16))
        _softmax_step(s, vpage[:, :HALF].astype(BF16), md_ref, ld_ref, accd_ref)
        km = kpage[:, HALF:]
        vm = vpage[:, HALF:].astype(BF16)
        s = _dot_nt(qm_ref[...], km.astype(BF16))
        ksum = jnp.sum(km, axis=0, keepdims=True)
        jb = p // pages_per_block
        first = p % pages_per_block == 0

        @pl.when(first)
        def _():
            mb_ref[jb] = jnp.full((rows, 1), NEG, F32)
            lb_ref[jb] = jnp.zeros((rows, 1), F32)
            accb_ref[jb] = jnp.zeros((rows, HALF), F32)
            ks_ref[...] = jnp.zeros(ks_ref.shape, F32)

        _softmax_step(s, vm, mb_ref.at[jb], lb_ref.at[jb], accb_ref.at[jb])
        ks_ref[...] += ksum

        @pl.when(p % pages_per_block == pages_per_block - 1)
        def _():
            g = jnp.sum(qmf_ref[...] * ks_ref[...], axis=1, keepdims=True) * (1.0 / MOBA_BLOCK)
            gate = gate_ref[...]
            gate_ref[...] = jnp.where(_iota(gate.shape, 1) == jb, g, gate)

    @pl.when(p == N_PAGES)
    def _():
        kpad = _pad_rows(kn_ref[0], PAGE_SIZE).astype(BF16)
        vpad = _pad_rows(vn_ref[0], PAGE_SIZE).astype(BF16)
        shape = (rows, PAGE_SIZE)
        causal = _iota(shape, 1) <= (_iota(shape, 0) & (t_new - 1))

        s = jnp.where(causal, _dot_nt(qd_ref[...], kpad[:, :HALF]), NEG)
        _softmax_step(s, vpad[:, :HALF], md_ref, ld_ref, accd_ref)
        od = accd_ref[...] / ld_ref[...]
        map_id = _iota(od.shape, 0) >> (t_new.bit_length() - 1)
        coef = jnp.where((map_id & 1) == 0, 1.0, -_lambda_value(lam_ref, lam_init))
        own = (map_id >> 1) == (_iota(od.shape, 1) >> ((2 * HEAD_DIM).bit_length() - 1))
        oa = _fold_heads(jnp.where(own, od * coef, 0.0), t_new)
        sg = sg_ref[...]
        for h in range(HALF // LANES):
            seg = oa[:, h * LANES:(h + 1) * LANES]
            o_ref[0, :, h * LANES:(h + 1) * LANES] = _rms(seg, sg) * (1.0 - lam_init)

        s = jnp.where(causal, _dot_nt(qm_ref[...], kpad[:, HALF:]), NEG)
        m_own = jnp.max(s, axis=1, keepdims=True)
        p_own = jnp.exp(s - m_own)
        l_own = jnp.sum(p_own, axis=1, keepdims=True)
        acc_own = _dot(p_own.astype(BF16), vpad[:, HALF:])
        gate = gate_ref[...]
        sel = _rank_select(gate, n_blocks, _iota(gate.shape, 1) < n_blocks)
        m_all = m_own
        for j in range(n_blocks):
            m_all = jnp.maximum(m_all, jnp.where(sel[:, j:j + 1], mb_ref[j], NEG))
        w_own = jnp.exp(m_own - m_all)
        num = w_own * acc_own
        den = w_own * l_own
        for j in range(n_blocks):
            w = jnp.where(sel[:, j:j + 1], jnp.exp(mb_ref[j] - m_all), 0.0)
            num = num + w * accb_ref[j]
            den = den + w * lb_ref[j]
        ob = jnp.where(_head_mask(num.shape, t_new, HEAD_DIM), num / den, 0.0)
        o_ref[0, :, HALF:] = _fold_heads(ob, t_new)


def _odd_decode_kernel(pt_ref, q_ref, kn_ref, vn_ref, ck_ref, cv_ref, o_ref, qs_ref, carry_ref, acc_ref,
                       *, t_new):
    del pt_ref
    p = pl.program_id(1)
    n_heads = D_MODEL // HEAD_DIM
    rows = n_heads * t_new

    @pl.when(p == 0)
    def _():
        qs_ref[...] = (_block_diag(q_ref[0], n_heads, HEAD_DIM) * SCALE).astype(BF16)
        carry_ref[...] = jnp.zeros(carry_ref.shape, F32)
        acc_ref[...] = jnp.zeros(acc_ref.shape, F32)
        kpad = _pad_rows(kn_ref[0], PAGE_SIZE).astype(BF16)
        vpad = _pad_rows(vn_ref[0], PAGE_SIZE).astype(BF16)
        shape = (rows, PAGE_SIZE)
        strict = _iota(shape, 1) < (_iota(shape, 0) & (t_new - 1))
        _sb_step(_dot_nt(qs_ref[...], kpad), strict, vpad, carry_ref, acc_ref)

    @pl.when(p > 0)
    def _():
        z = _dot_nt(qs_ref[...], ck_ref[...].astype(BF16))
        _sb_step(z, None, cv_ref[...].astype(BF16), carry_ref, acc_ref)

    @pl.when(p == N_PAGES)
    def _():
        acc = acc_ref[...]
        o_ref[0] = _fold_heads(jnp.where(_head_mask(acc.shape, t_new, HEAD_DIM), acc, 0.0), t_new)


def _decode_attention(kernel, q, k_new, v_new, cache_k, cache_v, page_table, layer, page_of_step,
                      extra_in, scratch):
    b, t_new, _ = q.shape
    new = pl.BlockSpec((1, t_new, D_MODEL), lambda bi, p, pt: (bi, 0, 0))
    page = pl.BlockSpec((None, None, PAGE_SIZE, D_MODEL),
                        lambda bi, p, pt: (layer, pt[bi, page_of_step(p)], 0, 0))
    extra_specs = [pl.BlockSpec(a.shape, lambda bi, p, pt: (0, 0)) for a in extra_in]
    return pl.pallas_call(
        kernel,
        grid_spec=pltpu.PrefetchScalarGridSpec(
            num_scalar_prefetch=1,
            grid=(b, N_PAGES + 1),
            in_specs=extra_specs + [new, new, new, page, page],
            out_specs=new,
            scratch_shapes=scratch,
        ),
        out_shape=jax.ShapeDtypeStruct((b, t_new, D_MODEL), F32),
        compiler_params=_params(("parallel", "arbitrary")),
    )(page_table, *extra_in, q, k_new, v_new, cache_k, cache_v)


def _decode_even_attention(q, k_new, v_new, cache_k, cache_v, page_table, layer, lam_vecs, subln, lam_init):
    t_new = q.shape[1]
    rows = (HALF // HEAD_DIM) * t_new
    n_blocks = PAST_LEN // MOBA_BLOCK
    scratch = [
        pltpu.VMEM((rows, HALF), BF16), pltpu.VMEM((rows, HALF), BF16), pltpu.VMEM((rows, HALF), F32),
        pltpu.VMEM((rows, 1), F32), pltpu.VMEM((rows, 1), F32), pltpu.VMEM((rows, HALF), F32),
        pltpu.VMEM((n_blocks, rows, 1), F32), pltpu.VMEM((n_blocks, rows, 1), F32),
        pltpu.VMEM((n_blocks, rows, HALF), F32),
        pltpu.VMEM((1, HALF), F32), pltpu.VMEM((rows, LANES), F32),
    ]
    return _decode_attention(
        functools.partial(_even_decode_kernel, lam_init=lam_init, t_new=t_new),
        q, k_new, v_new, cache_k, cache_v, page_table, layer,
        lambda p: jnp.minimum(p, N_PAGES - 1), (lam_vecs, subln), scratch)


def _decode_odd_attention(q, k_new, v_new, cache_k, cache_v, page_table, layer):
    t_new = q.shape[1]
    rows = (D_MODEL // HEAD_DIM) * t_new
    scratch = [pltpu.VMEM((rows, D_MODEL), BF16), pltpu.VMEM((rows, 1), F32), pltpu.VMEM((rows, D_MODEL), F32)]
    return _decode_attention(
        functools.partial(_odd_decode_kernel, t_new=t_new),
        q, k_new, v_new, cache_k, cache_v, page_table, layer,
        lambda p: N_PAGES - jnp.maximum(p, 1), (), scratch)


def _rope_tables(pos):
    half = HEAD_DIM // 2
    inv_freq = 1.0 / (ROPE_THETA ** (jnp.arange(half, dtype=F32) * (2.0 / HEAD_DIM)))
    ang = pos.astype(F32)[:, None] * inv_freq[None, :]
    cos, sin, zero = jnp.cos(ang), jnp.sin(ang), jnp.zeros_like(ang)
    reps = LANES // HEAD_DIM
    return (jnp.concatenate([cos, cos] * reps, axis=1),
            jnp.concatenate([-sin, zero] * reps, axis=1),
            jnp.concatenate([zero, sin] * reps, axis=1))


def _lambda_init(layer):
    return 0.8 - 0.6 * math.exp(-0.3 * layer)


def _trunk(x3, tables, cache, weights):
    b, t, _ = x3.shape
    n = b * t
    x = x3.reshape(n, D_MODEL)
    gains = weights["norm_gains"]
    new_k, new_v = [], []
    for l in range(DEPTH):
        g = [gains[l, i].reshape(1, D_MODEL) for i in range(6)]
        x = _ffn(x, g[0], g[1], weights["ffn1_w_in"], weights["ffn1_w_out"], l)
        even = l % 2 == 0
        q, k, v = _inproj(x, g[2], weights["mix_w_in"], l, tables, even)
        q3, k3, v3 = (a.reshape(b, t, D_MODEL) for a in (q, k, v))
        if even:
            lam_vecs = weights["diff_lambda"][l // 2]
            subln = weights["diff_subln"][l // 2].reshape(1, 2 * HEAD_DIM)
            if cache is None:
                o = _prompt_even_attention(q3, k3, v3, lam_vecs, subln, _lambda_init(l))
            else:
                o = _decode_even_attention(q3, k3, v3, *cache, l, lam_vecs, subln, _lambda_init(l))
        else:
            if cache is None:
                o = _prompt_odd_attention(q3, k3, v3)
            else:
                o = _decode_odd_attention(q3, k3, v3, *cache, l)
        x = _outproj(o.reshape(n, D_MODEL).astype(BF16), x, weights["mix_w_out"], g[3], l)
        x = _ffn(x, g[4], g[5], weights["ffn2_w_in"], weights["ffn2_w_out"], l)
        new_k.append(k3)
        new_v.append(v3)
    return x.reshape(b, t, D_MODEL), jnp.stack(new_k), jnp.stack(new_v)


def _even_layer_in_proj(w):
    return jnp.concatenate([w[:, c * HALF:(c + 1) * HALF] for c in (0, 3, 1, 4, 2, 5)], axis=1)


def kernel(x_prompt, x_sample, cache_k, cache_v, page_table, norm_gains, ffn1_w_in, ffn1_w_out, mix_w_in,
           mix_w_out, diff_lambda, diff_subln, ffn2_w_in, ffn2_w_out):
    mix_in = jnp.stack([_even_layer_in_proj(mix_w_in[l]) if l % 2 == 0 else mix_w_in[l] for l in range(DEPTH)])
    weights = {
        "norm_gains": norm_gains,
        "ffn1_w_in": ffn1_w_in.astype(BF16), "ffn1_w_out": ffn1_w_out.astype(BF16),
        "ffn2_w_in": ffn2_w_in.astype(BF16), "ffn2_w_out": ffn2_w_out.astype(BF16),
        "mix_w_in": mix_in.astype(BF16), "mix_w_out": mix_w_out.astype(BF16),
        "diff_lambda": diff_lambda, "diff_subln": diff_subln,
    }
    seq = x_prompt.shape[1]
    t_new = x_sample.shape[1]
    prompt_tables = _rope_tables(jnp.arange(seq, dtype=jnp.int32))
    sample_pos = PAST_LEN + (jnp.arange(FFN_TM, dtype=jnp.int32) % t_new)
    sample_tables = _rope_tables(sample_pos)

    y_prompt, k_prompt, v_prompt = _trunk(x_prompt, prompt_tables, None, weights)
    y_sample, k_sample, v_sample = _trunk(x_sample, sample_tables, (cache_k, cache_v, page_table), weights)
    return (y_prompt, y_sample, k_prompt, v_prompt, k_sample, v_sample)
```

```python
import functools
import math

import jax
import jax.numpy as jnp
from jax import lax
from jax.experimental import pallas as pl
from jax.experimental.pallas import tpu as pltpu

F32 = jnp.float32
BF16 = jnp.bfloat16

D_MODEL = 1024
HEAD_DIM = 64
D_FF = 2816
DEPTH = 4
PAGE_SIZE = 128
PAST_LEN = 2048
N_PAGES = PAST_LEN // PAGE_SIZE
MOBA_BLOCK = 256
MOBA_TOPK = 3
ROPE_THETA = 10000.0
NORM_EPS = 1e-6
SCALE = HEAD_DIM ** -0.5
HALF = D_MODEL // 2
LANES = 128
NEG = -1e30

VMEM_LIMIT = 56 * 1024 * 1024

FFN_TM = 512
FFN_TF = 1408
ATT_TQ = 256
SB_TK = 128
SB_KB = 256
SB_TQ = 512
DEC_PAGES = 4


def _params(sem):
    return pltpu.CompilerParams(dimension_semantics=sem, vmem_limit_bytes=VMEM_LIMIT)


def _rms(x, g):
    return x * lax.rsqrt(jnp.mean(x * x, axis=-1, keepdims=True) + NORM_EPS) * g


def _dot_nt(a, b):
    return lax.dot_general(a, b, (((1,), (1,)), ((), ())), preferred_element_type=F32)


def _dot(a, b):
    return jnp.dot(a, b, preferred_element_type=F32)


def _iota(shape, dim):
    return lax.broadcasted_iota(jnp.int32, shape, dim)


def _ffn_kernel(x_ref, gpre_ref, gpost_ref, wg_ref, wu_ref, wo_ref, o_ref, xn_ref, acc_ref):
    j = pl.program_id(1)

    @pl.when(j == 0)
    def _():
        xn_ref[...] = _rms(x_ref[...], gpre_ref[...]).astype(BF16)

    xn = xn_ref[...]
    gate = _dot(xn, wg_ref[...])
    up = _dot(xn, wu_ref[...])
    h = (gate * jax.nn.sigmoid(gate) * up).astype(BF16)
    part = _dot(h, wo_ref[...])

    @pl.when(j == 0)
    def _():
        acc_ref[...] = part

    @pl.when(j > 0)
    def _():
        acc_ref[...] += part

    @pl.when(j == pl.num_programs(1) - 1)
    def _():
        o_ref[...] = x_ref[...] + 0.5 * _rms(acc_ref[...], gpost_ref[...])


def _ffn(x, g_pre, g_post, w_in, w_out, layer):
    n = x.shape[0]
    nf = D_FF // FFN_TF
    return pl.pallas_call(
        _ffn_kernel,
        grid=(n // FFN_TM, nf),
        in_specs=[
            pl.BlockSpec((FFN_TM, D_MODEL), lambda i, j: (i, 0)),
            pl.BlockSpec((1, D_MODEL), lambda i, j: (0, 0)),
            pl.BlockSpec((1, D_MODEL), lambda i, j: (0, 0)),
            pl.BlockSpec((None, D_MODEL, FFN_TF), lambda i, j: (layer, 0, j)),
            pl.BlockSpec((None, D_MODEL, FFN_TF), lambda i, j: (layer, 0, j + nf)),
            pl.BlockSpec((None, FFN_TF, D_MODEL), lambda i, j: (layer, j, 0)),
        ],
        out_specs=pl.BlockSpec((FFN_TM, D_MODEL), lambda i, j: (i, 0)),
        out_shape=jax.ShapeDtypeStruct((n, D_MODEL), F32),
        scratch_shapes=[pltpu.VMEM((FFN_TM, D_MODEL), BF16), pltpu.VMEM((FFN_TM, D_MODEL), F32)],
        compiler_params=_params(("parallel", "arbitrary")),
    )(x, g_pre, g_post, w_in, w_in, w_out)


def _inproj_kernel(x_ref, g_ref, w_ref, cos_ref, sina_ref, sinb_ref, q_ref, k_ref, v_ref, *, rotary):
    xn = _rms(x_ref[...], g_ref[...]).astype(BF16)
    outs = (q_ref, k_ref, v_ref)
    for c in range(3):
        y = _dot(xn, w_ref[:, c * D_MODEL:(c + 1) * D_MODEL])
        if rotary and c < 2:
            cos, sina, sinb = cos_ref[...], sina_ref[...], sinb_ref[...]
            for p in range(D_MODEL // LANES):
                yp = y[:, p * LANES:(p + 1) * LANES]
                fwd = pltpu.roll(yp, LANES - HEAD_DIM // 2, 1)
                bwd = pltpu.roll(yp, HEAD_DIM // 2, 1)
                outs[c][:, p * LANES:(p + 1) * LANES] = yp * cos + fwd * sina + bwd * sinb
        else:
            outs[c][...] = y


def _inproj(x, g, w, layer, tables, rotary):
    n = x.shape[0]
    cos, sina, sinb = tables
    nt = cos.shape[0] // FFN_TM
    tab = pl.BlockSpec((FFN_TM, LANES), lambda i: (i % nt, 0))
    row = pl.BlockSpec((FFN_TM, D_MODEL), lambda i: (i, 0))
    return pl.pallas_call(
        functools.partial(_inproj_kernel, rotary=rotary),
        grid=(n // FFN_TM,),
        in_specs=[
            row,
            pl.BlockSpec((1, D_MODEL), lambda i: (0, 0)),
            pl.BlockSpec((None, D_MODEL, 3 * D_MODEL), lambda i: (layer, 0, 0)),
            tab, tab, tab,
        ],
        out_specs=[row, row, row],
        out_shape=[jax.ShapeDtypeStruct((n, D_MODEL), F32)] * 3,
        compiler_params=_params(("parallel",)),
    )(x, g, w, cos, sina, sinb)


def _outproj_kernel(o_ref, x_ref, w_ref, g_ref, y_ref):
    m = _dot(o_ref[...], w_ref[...])
    y_ref[...] = x_ref[...] + _rms(m, g_ref[...])


def _outproj(o, x, w, g, layer):
    n = x.shape[0]
    row = pl.BlockSpec((FFN_TM, D_MODEL), lambda i: (i, 0))
    return pl.pallas_call(
        _outproj_kernel,
        grid=(n // FFN_TM,),
        in_specs=[
            row, row,
            pl.BlockSpec((None, D_MODEL, D_MODEL), lambda i: (layer, 0, 0)),
            pl.BlockSpec((1, D_MODEL), lambda i: (0, 0)),
        ],
        out_specs=row,
        out_shape=jax.ShapeDtypeStruct((n, D_MODEL), F32),
        compiler_params=_params(("parallel",)),
    )(o, x, w, g)


def _stack_heads(x):
    low = _iota(x.shape, 1) < HEAD_DIM
    return jnp.concatenate([jnp.where(low, x, 0.0), jnp.where(low, 0.0, x)], axis=0)


def _softmax_step(s, vb, m_ref, l_ref, acc_ref):
    m_old = m_ref[...]
    m_new = jnp.maximum(m_old, jnp.max(s, axis=1, keepdims=True))
    alpha = jnp.exp(m_old - m_new)
    p = jnp.exp(s - m_new)
    l_ref[...] = alpha * l_ref[...] + jnp.sum(p, axis=1, keepdims=True)
    acc_ref[...] = alpha * acc_ref[...] + _dot(p.astype(BF16), vb)
    m_ref[...] = m_new


def _softmax_step_t(s, m_ref, l_ref):
    m_old = m_ref[...]
    m_new = jnp.maximum(m_old, jnp.max(s, axis=0, keepdims=True))
    alpha = jnp.exp(m_old - m_new)
    p = jnp.exp(s - m_new)
    l_ref[...] = alpha * l_ref[...] + jnp.sum(p, axis=0, keepdims=True)
    m_ref[...] = m_new
    return alpha, p.astype(BF16)


def _lambda_value(lam_ref, lam_init):
    lv = lam_ref[...]
    a = jnp.sum(lv[0:1] * lv[1:2], axis=1, keepdims=True)
    b = jnp.sum(lv[2:3] * lv[3:4], axis=1, keepdims=True)
    return jnp.exp(a) - jnp.exp(b) + lam_init


def _sb_weights():
    shape = (2 * SB_TK, 2 * SB_TK)
    j = _iota(shape, 0) & (SB_TK - 1)
    s = _iota(shape, 1)
    return jnp.where((s >= SB_TK) | (j >= s), 1.0, 0.0).astype(BF16)


def _sb_suffix(sp, w):
    hi = sp.astype(BF16)
    lo = (sp - hi.astype(F32)).astype(BF16)
    r = _dot(jnp.concatenate([hi, lo], axis=1), w)
    return r[:, :SB_TK], r[:, SB_TK:]


def _softplus(z):
    return jnp.maximum(z, 0.0) + jnp.log(1.0 + jnp.exp(-jnp.abs(z)))


def _rank_select_t(gate, eligible):
    blk = _iota(gate.shape, 0)
    g = jnp.where(eligible, gate, -jnp.inf)
    cnt = jnp.zeros(gate.shape, jnp.int32)
    for jp in range(gate.shape[0]):
        row = g[jp:jp + 1, :]
        beats = (row > g) | ((row == g) & (jp < blk))
        cnt = cnt + jnp.where(beats, 1, 0)
    return (cnt < MOBA_TOPK) & eligible


def _rank_select(gate, n_blocks, eligible):
    lane = _iota(gate.shape, 1)
    g = jnp.where(eligible, gate, -jnp.inf)
    cnt = jnp.zeros(gate.shape, jnp.int32)
    for jp in range(n_blocks):
        col = g[:, jp:jp + 1]
        beats = (col > g) | ((col == g) & (jp < lane))
        cnt = cnt + jnp.where(beats, 1, 0)
    return (cnt < MOBA_TOPK) & eligible


def _store_v_transposed(v_ref, vt_ref):
    tk = vt_ref.shape[2]
    for c in range(vt_ref.shape[0]):
        vt_ref[c] = v_ref[0, c * tk:(c + 1) * tk, :].T.astype(BF16)


def _causal_t(shape, tk):
    return (_iota(shape, 0) & (tk - 1)) <= _iota(shape, 1)


def _pipelined_blocks(i, scores, consume, st_ref):
    st_ref[...] = scores(i, True)

    def body(j, c):
        pending = st_ref[...]
        st_ref[...] = scores(j, False)
        consume(pending, jnp.where(j == 0, i, j - 1))
        return c

    lax.fori_loop(0, i, body, 0)
    consume(st_ref[...], jnp.maximum(i - 1, 0))


def _diff_prompt_kernel(lam_ref, sg_ref, q_ref, k_ref, v_ref, o_ref, vt_ref, st_ref, m_ref, l_ref, acc_ref, *,
                        lam_init):
    i = pl.program_id(2)
    tq = ATT_TQ

    @pl.when(i == 0)
    def _():
        _store_v_transposed(v_ref, vt_ref)

    qs = (q_ref[0] * SCALE).astype(BF16)
    m_ref[...] = jnp.full(m_ref.shape, NEG, F32)
    l_ref[...] = jnp.zeros(l_ref.shape, F32)
    acc_ref[...] = jnp.zeros(acc_ref.shape, F32)

    def scores(j, masked):
        kb = k_ref[0, pl.ds(pl.multiple_of(j * tq, tq), tq), :].astype(BF16)
        st = _dot_nt(_stack_heads(kb), qs)
        return jnp.where(_causal_t(st.shape, tq), st, NEG) if masked else st

    def consume(st, j):
        vt = vt_ref[j]
        for mp in range(2):
            alpha, p = _softmax_step_t(st[mp * tq:(mp + 1) * tq], m_ref.at[mp], l_ref.at[mp])
            acc_ref[mp] = alpha * acc_ref[mp] + _dot(vt, p)

    _pipelined_blocks(i, scores, consume, st_ref)

    a = acc_ref[0] / l_ref[0] - _lambda_value(lam_ref, lam_init) * (acc_ref[1] / l_ref[1])
    y = a * lax.rsqrt(jnp.mean(a * a, axis=0, keepdims=True) + NORM_EPS) * sg_ref[...] * (1.0 - lam_init)
    o_ref[0] = y.T.astype(o_ref.dtype)


def _moba_prompt_kernel(q_ref, k_ref, v_ref, o_ref, vt_ref, st_ref, kbar_ref, bias_ref, m_ref, l_ref, acc_ref):
    i = pl.program_id(2)
    tq = ATT_TQ
    n_blocks = k_ref.shape[1] // MOBA_BLOCK

    @pl.when(i == 0)
    def _():
        _store_v_transposed(v_ref, vt_ref)
        low = _iota((1, LANES), 1) < HEAD_DIM
        for jb in range(n_blocks):
            blk = k_ref[0, jb * MOBA_BLOCK:(jb + 1) * MOBA_BLOCK, :]
            kbar = jnp.sum(blk, axis=0, keepdims=True) * (1.0 / MOBA_BLOCK)
            kbar_ref[0, jb:jb + 1, :] = jnp.where(low, kbar, 0.0)
            kbar_ref[1, jb:jb + 1, :] = jnp.where(low, 0.0, kbar)

    q = q_ref[0]
    for h in range(2):
        gate = lax.dot_general(kbar_ref[h], q, (((1,), (1,)), ((), ())),
                               precision=lax.Precision.HIGHEST, preferred_element_type=F32)
        sel = _rank_select_t(gate, _iota(gate.shape, 0) < i)
        bias_ref[h] = jnp.where(sel, 0.0, NEG)
    qs = (q * SCALE).astype(BF16)

    m_ref[...] = jnp.full(m_ref.shape, NEG, F32)
    l_ref[...] = jnp.zeros(l_ref.shape, F32)
    acc_ref[...] = jnp.zeros(acc_ref.shape, F32)
    head_a = _iota((LANES, tq), 0) < HEAD_DIM

    def scores(j, diagonal):
        kb = k_ref[0, pl.ds(pl.multiple_of(j * tq, tq), tq), :].astype(BF16)
        st = _dot_nt(_stack_heads(kb), qs)
        if diagonal:
            return jnp.where(_causal_t(st.shape, tq), st, NEG)
        return jnp.concatenate([st[h * tq:(h + 1) * tq] + bias_ref[h, pl.ds(j, 1), :] for h in range(2)], axis=0)

    def consume(st, j):
        alphas, ps = [], []
        for h in range(2):
            alpha, p = _softmax_step_t(st[h * tq:(h + 1) * tq], m_ref.at[h], l_ref.at[h])
            alphas.append(alpha)
            ps.append(p)
        vt = vt_ref[j]
        keep = _iota(vt.shape, 0) < HEAD_DIM
        vbd = jnp.concatenate([jnp.where(keep, vt, 0.0), jnp.where(keep, 0.0, vt)], axis=1)
        alpha = jnp.where(head_a, alphas[0], alphas[1])
        acc_ref[...] = alpha * acc_ref[...] + _dot(vbd, jnp.concatenate(ps, axis=0))

    _pipelined_blocks(i, scores, consume, st_ref)

    o = acc_ref[...] / jnp.where(head_a, l_ref[0], l_ref[1])
    o_ref[0] = o.T.astype(o_ref.dtype)


def _sb_scores(qs, kb, w, strict):
    z = _dot_nt(qs, _stack_heads(kb))
    if strict is not None:
        z = jnp.where(strict, z, NEG)
    sp = _softplus(z)
    parts = [_sb_suffix(sp[:, g * SB_TK:(g + 1) * SB_TK], w) for g in range(z.shape[1] // SB_TK)]
    suffix = jnp.concatenate([p[0] for p in parts], axis=1)
    rowsum = jnp.concatenate([p[1] for p in parts], axis=1)
    return z - suffix, rowsum


def _sb_apply(t, rowsum, vb, carry_ref, acc_ref):
    groups = t.shape[1] // (2 * SB_TK)
    carry = carry_ref[...]
    weights = [None] * (2 * groups)
    carries = []
    for h in range(2):
        c = carry[:, h * SB_TK:(h + 1) * SB_TK]
        for u in reversed(range(groups)):
            g = h * groups + u
            weights[g] = jnp.exp(t[:, g * SB_TK:(g + 1) * SB_TK] - c).astype(BF16)
            c = c + rowsum[:, g * SB_TK:(g + 1) * SB_TK]
        carries.append(c)
    acc_ref[...] += _dot(jnp.concatenate(weights, axis=1), _stack_heads(vb))
    carry_ref[...] = jnp.concatenate(carries, axis=1)


def _sb_prompt_kernel(q_ref, k_ref, v_ref, o_ref, t_ref, rs_ref, carry_ref, acc_ref):
    i = pl.program_id(2)
    tq, tk = SB_TQ, SB_KB
    sub = tq // tk
    qs = (q_ref[0] * SCALE).astype(BF16)
    w = _sb_weights()
    carry_ref[...] = jnp.zeros(carry_ref.shape, F32)
    acc_ref[...] = jnp.zeros(acc_ref.shape, F32)

    def scores(j, strict):
        kb = k_ref[0, pl.ds(pl.multiple_of(j * tk, tk), tk), :].astype(BF16)
        return _sb_scores(qs, kb, w, strict)

    def apply(t, rowsum, j):
        vb = v_ref[0, pl.ds(pl.multiple_of(j * tk, tk), tk), :].astype(BF16)
        _sb_apply(t, rowsum, vb, carry_ref, acc_ref)

    shape = (tq, 2 * tk)
    pending = None
    for u in reversed(range(sub)):
        kpos = u * tk + (_iota(shape, 1) & (tk - 1))
        fresh = scores(i * sub + u, kpos < _iota(shape, 0))
        if pending is not None:
            apply(*pending, i * sub + u + 1)
        pending = fresh
    t_ref[...], rs_ref[...] = pending

    def body(n, c):
        j = i * sub - 1 - n
        t, rowsum = t_ref[...], rs_ref[...]
        t_ref[...], rs_ref[...] = scores(j, None)
        apply(t, rowsum, j + 1)
        return c

    lax.fori_loop(0, i * sub, body, 0)
    apply(t_ref[...], rs_ref[...], 0)
    o_ref[0] = acc_ref[...].astype(o_ref.dtype)


def _prompt_attention_call(kernel, q, k, v, col0, n_col, extra_in, extra_specs, scratch, tq=ATT_TQ):
    b, t, _ = q.shape
    qspec = pl.BlockSpec((1, tq, LANES), lambda bi, c, i: (bi, i, col0 + c))
    kvspec = pl.BlockSpec((1, t, LANES), lambda bi, c, i: (bi, 0, col0 + c))
    return pl.pallas_call(
        kernel,
        grid=(b, n_col, t // tq),
        in_specs=list(extra_specs) + [qspec, kvspec, kvspec],
        out_specs=pl.BlockSpec((1, tq, LANES), lambda bi, c, i: (bi, i, c)),
        out_shape=jax.ShapeDtypeStruct((b, t, n_col * LANES), BF16),
        scratch_shapes=scratch,
        compiler_params=_params(("parallel", "parallel", "arbitrary")),
    )(*extra_in, q, k, v)


def _prompt_even_attention(q, k, v, lam_vecs, subln, lam_init):
    t = q.shape[1]
    tq = ATT_TQ
    whole = lambda shape: pl.BlockSpec(shape, lambda bi, c, i: (0, 0))
    vt = [pltpu.VMEM((t // tq, LANES, tq), BF16), pltpu.VMEM((2 * tq, tq), F32)]
    stats = [pltpu.VMEM((2, 1, tq), F32), pltpu.VMEM((2, 1, tq), F32)]
    sg = jnp.broadcast_to(subln.reshape(2 * HEAD_DIM, 1), (2 * HEAD_DIM, tq))
    o_a = _prompt_attention_call(
        functools.partial(_diff_prompt_kernel, lam_init=lam_init), q, k, v, 0, HALF // LANES,
        (lam_vecs, sg), (whole(lam_vecs.shape), whole(sg.shape)),
        vt + stats + [pltpu.VMEM((2, LANES, tq), F32)])
    n_blocks = t // MOBA_BLOCK
    o_b = _prompt_attention_call(
        _moba_prompt_kernel, q, k, v, HALF // LANES, HALF // LANES, (), (),
        vt + [pltpu.VMEM((2, n_blocks, LANES), F32), pltpu.VMEM((2, n_blocks, tq), F32)] + stats
        + [pltpu.VMEM((LANES, tq), F32)])
    return jnp.concatenate([o_a, o_b], axis=-1)


def _prompt_odd_attention(q, k, v):
    return _prompt_attention_call(
        _sb_prompt_kernel, q, k, v, 0, D_MODEL // LANES, (), (),
        [pltpu.VMEM((SB_TQ, 2 * SB_KB), F32)] * 2 + [pltpu.VMEM((SB_TQ, 2 * SB_TK), F32),
                                                     pltpu.VMEM((SB_TQ, LANES), F32)], tq=SB_TQ)


def _block_diag(q, n_heads, width):
    t = q.shape[0]
    tiled = jnp.concatenate([q] * n_heads, axis=0)
    keep = _head_mask(tiled.shape, t, width)
    return jnp.where(keep, tiled, 0.0)


def _head_mask(shape, rows_per_head, width):
    shift_r = rows_per_head.bit_length() - 1
    shift_c = width.bit_length() - 1
    return (_iota(shape, 0) >> shift_r) == (_iota(shape, 1) >> shift_c)


def _fold_heads(x, t):
    out = x[0:t]
    for h in range(1, x.shape[0] // t):
        out = out + x[h * t:(h + 1) * t]
    return out


def _pad_rows(x, rows):
    return jnp.concatenate([x, jnp.zeros((rows - x.shape[0], x.shape[1]), x.dtype)], axis=0)


def _even_decode_kernel(pt_ref, lam_ref, sg_ref, q_ref, kn_ref, vn_ref, *rest, lam_init, t_new):
    del pt_ref
    ck_refs, cv_refs = rest[:DEC_PAGES], rest[DEC_PAGES:2 * DEC_PAGES]
    (o_ref, qd_ref, qm_ref, qmf_ref, md_ref, ld_ref, accd_ref,
     mb_ref, lb_ref, accb_ref, gate_ref) = rest[2 * DEC_PAGES:]
    p = pl.program_id(1)
    n_steps = N_PAGES // DEC_PAGES
    n_maps = HALF // HEAD_DIM
    rows = n_maps * t_new
    blocks_per_step = DEC_PAGES * PAGE_SIZE // MOBA_BLOCK
    n_blocks = PAST_LEN // MOBA_BLOCK

    @pl.when(p == 0)
    def _():
        q = q_ref[0]
        qd_ref[...] = (_block_diag(q[:, :HALF], n_maps, HEAD_DIM) * SCALE).astype(BF16)
        qmf = _block_diag(q[:, HALF:], n_maps, HEAD_DIM)
        qmf_ref[...] = qmf
        qm_ref[...] = (qmf * SCALE).astype(BF16)
        md_ref[...] = jnp.full(md_ref.shape, NEG, F32)
        ld_ref[...] = jnp.zeros(ld_ref.shape, F32)
        accd_ref[...] = jnp.zeros(accd_ref.shape, F32)
        gate_ref[...] = jnp.zeros(gate_ref.shape, F32)

    @pl.when(p < n_steps)
    def _():
        kcat = jnp.concatenate([r[...] for r in ck_refs], axis=0)
        vcat = jnp.concatenate([r[...] for r in cv_refs], axis=0)
        s = _dot_nt(qd_ref[...], kcat[:, :HALF].astype(BF16))
        _softmax_step(s, vcat[:, :HALF].astype(BF16), md_ref, ld_ref, accd_ref)
        km = kcat[:, HALF:]
        vm = vcat[:, HALF:].astype(BF16)
        s = _dot_nt(qm_ref[...], km.astype(BF16))
        gate = gate_ref[...]
        for u in range(blocks_per_step):
            jb = p * blocks_per_step + u
            sb = s[:, u * MOBA_BLOCK:(u + 1) * MOBA_BLOCK]
            m = jnp.max(sb, axis=1, keepdims=True)
            pe = jnp.exp(sb - m)
            mb_ref[jb] = m
            lb_ref[jb] = jnp.sum(pe, axis=1, keepdims=True)
            accb_ref[jb] = _dot(pe.astype(BF16), vm[u * MOBA_BLOCK:(u + 1) * MOBA_BLOCK])
            kbar = jnp.sum(km[u * MOBA_BLOCK:(u + 1) * MOBA_BLOCK], axis=0, keepdims=True) * (1.0 / MOBA_BLOCK)
            g = jnp.sum(qmf_ref[...] * kbar, axis=1, keepdims=True)
            gate = jnp.where(_iota(gate.shape, 1) == jb, g, gate)
        gate_ref[...] = gate

    @pl.when(p == n_steps)
    def _():
        kpad = _pad_rows(kn_ref[0], PAGE_SIZE).astype(BF16)
        vpad = _pad_rows(vn_ref[0], PAGE_SIZE).astype(BF16)
        shape = (rows, PAGE_SIZE)
        causal = _iota(shape, 1) <= (_iota(shape, 0) & (t_new - 1))

        s = jnp.where(causal, _dot_nt(qd_ref[...], kpad[:, :HALF]), NEG)
        _softmax_step(s, vpad[:, :HALF], md_ref, ld_ref, accd_ref)
        od = accd_ref[...] / ld_ref[...]
        map_id = _iota(od.shape, 0) >> (t_new.bit_length() - 1)
        coef = jnp.where((map_id & 1) == 0, 1.0, -_lambda_value(lam_ref, lam_init))
        own = (map_id >> 1) == (_iota(od.shape, 1) >> ((2 * HEAD_DIM).bit_length() - 1))
        oa = _fold_heads(jnp.where(own, od * coef, 0.0), t_new)
        sg = sg_ref[...]
        for h in range(HALF // LANES):
            seg = oa[:, h * LANES:(h + 1) * LANES]
            o_ref[0, :, h * LANES:(h + 1) * LANES] = _rms(seg, sg) * (1.0 - lam_init)

        s = jnp.where(causal, _dot_nt(qm_ref[...], kpad[:, HALF:]), NEG)
        m_own = jnp.max(s, axis=1, keepdims=True)
        p_own = jnp.exp(s - m_own)
        l_own = jnp.sum(p_own, axis=1, keepdims=True)
        acc_own = _dot(p_own.astype(BF16), vpad[:, HALF:])
        gate = gate_ref[...]
        sel = _rank_select(gate, n_blocks, _iota(gate.shape, 1) < n_blocks)
        m_all = m_own
        for j in range(n_blocks):
            m_all = jnp.maximum(m_all, jnp.where(sel[:, j:j + 1], mb_ref[j], NEG))
        w_own = jnp.exp(m_own - m_all)
        num = w_own * acc_own
        den = w_own * l_own
        for j in range(n_blocks):
            w = jnp.where(sel[:, j:j + 1], jnp.exp(mb_ref[j] - m_all), 0.0)
            num = num + w * accb_ref[j]
            den = den + w * lb_ref[j]
        ob = jnp.where(_head_mask(num.shape, t_new, HEAD_DIM), num / den, 0.0)
        o_ref[0, :, HALF:] = _fold_heads(ob, t_new)


def _odd_decode_kernel(pt_ref, q_ref, kn_ref, vn_ref, *rest, t_new):
    del pt_ref
    ck_refs, cv_refs = rest[:DEC_PAGES], rest[DEC_PAGES:2 * DEC_PAGES]
    o_ref, qs_ref, carry_ref, acc_ref = rest[2 * DEC_PAGES:]
    p = pl.program_id(1)
    n_heads = D_MODEL // HEAD_DIM
    rows = n_heads * t_new

    def page(k_bf, v_bf, strict, w):
        z = _dot_nt(qs_ref[...], k_bf)
        sp = _softplus(z)
        if strict is not None:
            sp = jnp.where(strict, sp, 0.0)
        suffix, rowsum = _sb_suffix(sp, w)
        return z, suffix, rowsum

    @pl.when(p == 0)
    def _():
        qs_ref[...] = (_block_diag(q_ref[0], n_heads, HEAD_DIM) * SCALE).astype(BF16)
        acc_ref[...] = jnp.zeros(acc_ref.shape, F32)
        kpad = _pad_rows(kn_ref[0], PAGE_SIZE).astype(BF16)
        vpad = _pad_rows(vn_ref[0], PAGE_SIZE).astype(BF16)
        shape = (rows, PAGE_SIZE)
        strict = _iota(shape, 1) < (_iota(shape, 0) & (t_new - 1))
        z, suffix, rowsum = page(kpad, vpad, strict, _sb_weights())
        a = jnp.where(strict, jnp.exp(z - suffix), 0.0)
        acc_ref[...] += _dot(a.astype(BF16), vpad)
        carry_ref[...] = rowsum

    @pl.when(p > 0)
    def _():
        w = _sb_weights()
        carry = carry_ref[...]
        weights = []
        for u in range(DEC_PAGES):
            z, suffix, rowsum = page(ck_refs[u][...].astype(BF16), None, None, w)
            weights.append(jnp.exp(z - suffix - carry).astype(BF16))
            carry = carry + rowsum
        carry_ref[...] = carry
        vcat = jnp.concatenate([r[...] for r in cv_refs], axis=0).astype(BF16)
        acc_ref[...] += _dot(jnp.concatenate(weights, axis=1), vcat)

    @pl.when(p == N_PAGES // DEC_PAGES)
    def _():
        acc = acc_ref[...]
        o_ref[0] = _fold_heads(jnp.where(_head_mask(acc.shape, t_new, HEAD_DIM), acc, 0.0), t_new)


def _decode_attention(kernel, q, k_new, v_new, cache_k, cache_v, page_table, layer, page_of, extra_in, scratch):
    b, t_new, _ = q.shape
    new = pl.BlockSpec((1, t_new, D_MODEL), lambda bi, p, pt: (bi, 0, 0))

    def page(u):
        return pl.BlockSpec((None, None, PAGE_SIZE, D_MODEL),
                            lambda bi, p, pt: (layer, pt[bi, page_of(p, u)], 0, 0))

    pages = [page(u) for u in range(DEC_PAGES)]
    extra_specs = [pl.BlockSpec(a.shape, lambda bi, p, pt: (0, 0)) for a in extra_in]
    return pl.pallas_call(
        kernel,
        grid_spec=pltpu.PrefetchScalarGridSpec(
            num_scalar_prefetch=1,
            grid=(b, N_PAGES // DEC_PAGES + 1),
            in_specs=extra_specs + [new, new, new] + pages + pages,
            out_specs=new,
            scratch_shapes=scratch,
        ),
        out_shape=jax.ShapeDtypeStruct((b, t_new, D_MODEL), F32),
        compiler_params=_params(("parallel", "arbitrary")),
    )(page_table, *extra_in, q, k_new, v_new, *([cache_k] * DEC_PAGES), *([cache_v] * DEC_PAGES))


def _decode_even_attention(q, k_new, v_new, cache_k, cache_v, page_table, layer, lam_vecs, subln, lam_init):
    t_new = q.shape[1]
    rows = (HALF // HEAD_DIM) * t_new
    n_blocks = PAST_LEN // MOBA_BLOCK
    n_steps = N_PAGES // DEC_PAGES
    scratch = [
        pltpu.VMEM((rows, HALF), BF16), pltpu.VMEM((rows, HALF), BF16), pltpu.VMEM((rows, HALF), F32),
        pltpu.VMEM((rows, 1), F32), pltpu.VMEM((rows, 1), F32), pltpu.VMEM((rows, HALF), F32),
        pltpu.VMEM((n_blocks, rows, 1), F32), pltpu.VMEM((n_blocks, rows, 1), F32),
        pltpu.VMEM((n_blocks, rows, HALF), F32),
        pltpu.VMEM((rows, LANES), F32),
    ]
    page_of = lambda p, u: jnp.minimum(p, n_steps - 1) * DEC_PAGES + u
    return _decode_attention(
        functools.partial(_even_decode_kernel, lam_init=lam_init, t_new=t_new),
        q, k_new, v_new, cache_k, cache_v, page_table, layer, page_of, (lam_vecs, subln), scratch)


def _decode_odd_attention(q, k_new, v_new, cache_k, cache_v, page_table, layer):
    t_new = q.shape[1]
    rows = (D_MODEL // HEAD_DIM) * t_new
    scratch = [pltpu.VMEM((rows, D_MODEL), BF16), pltpu.VMEM((rows, LANES), F32), pltpu.VMEM((rows, D_MODEL), F32)]
    page_of = lambda p, u: N_PAGES - (jnp.maximum(p, 1) - 1) * DEC_PAGES - 1 - u
    return _decode_attention(
        functools.partial(_odd_decode_kernel, t_new=t_new),
        q, k_new, v_new, cache_k, cache_v, page_table, layer, page_of, (), scratch)


def _rope_tables(pos):
    half = HEAD_DIM // 2
    inv_freq = 1.0 / (ROPE_THETA ** (jnp.arange(half, dtype=F32) * (2.0 / HEAD_DIM)))
    ang = pos.astype(F32)[:, None] * inv_freq[None, :]
    cos, sin, zero = jnp.cos(ang), jnp.sin(ang), jnp.zeros_like(ang)
    reps = LANES // HEAD_DIM
    return (jnp.concatenate([cos, cos] * reps, axis=1),
            jnp.concatenate([-sin, zero] * reps, axis=1),
            jnp.concatenate([zero, sin] * reps, axis=1))


def _lambda_init(layer):
    return 0.8 - 0.6 * math.exp(-0.3 * layer)


def _trunk(x3, tables, cache, weights):
    b, t, _ = x3.shape
    n = b * t
    x = x3.reshape(n, D_MODEL)
    gains = weights["norm_gains"]
    new_k, new_v = [], []
    for l in range(DEPTH):
        g = [gains[l, i].reshape(1, D_MODEL) for i in range(6)]
        x = _ffn(x, g[0], g[1], weights["ffn1_w_in"], weights["ffn1_w_out"], l)
        even = l % 2 == 0
        q, k, v = _inproj(x, g[2], weights["mix_w_in"], l, tables, even)
        q3, k3, v3 = (a.reshape(b, t, D_MODEL) for a in (q, k, v))
        if even:
            lam_vecs = weights["diff_lambda"][l // 2]
            subln = weights["diff_subln"][l // 2].reshape(1, 2 * HEAD_DIM)
            if cache is None:
                o = _prompt_even_attention(q3, k3, v3, lam_vecs, subln, _lambda_init(l))
            else:
                o = _decode_even_attention(q3, k3, v3, *cache, l, lam_vecs, subln, _lambda_init(l))
        else:
            if cache is None:
                o = _prompt_odd_attention(q3, k3, v3)
            else:
                o = _decode_odd_attention(q3, k3, v3, *cache, l)
        x = _outproj(o.reshape(n, D_MODEL).astype(BF16), x, weights["mix_w_out"], g[3], l)
        x = _ffn(x, g[4], g[5], weights["ffn2_w_in"], weights["ffn2_w_out"], l)
        new_k.append(k3)
        new_v.append(v3)
    return x.reshape(b, t, D_MODEL), jnp.stack(new_k), jnp.stack(new_v)


def _even_layer_in_proj(w):
    return jnp.concatenate([w[:, c * HALF:(c + 1) * HALF] for c in (0, 3, 1, 4, 2, 5)], axis=1)


def kernel(x_prompt, x_sample, cache_k, cache_v, page_table, norm_gains, ffn1_w_in, ffn1_w_out, mix_w_in,
           mix_w_out, diff_lambda, diff_subln, ffn2_w_in, ffn2_w_out):
    mix_in = jnp.stack([_even_layer_in_proj(mix_w_in[l]) if l % 2 == 0 else mix_w_in[l] for l in range(DEPTH)])
    weights = {
        "norm_gains": norm_gains,
        "ffn1_w_in": ffn1_w_in.astype(BF16), "ffn1_w_out": ffn1_w_out.astype(BF16),
        "ffn2_w_in": ffn2_w_in.astype(BF16), "ffn2_w_out": ffn2_w_out.astype(BF16),
        "mix_w_in": mix_in.astype(BF16), "mix_w_out": mix_w_out.astype(BF16),
        "diff_lambda": diff_lambda, "diff_subln": diff_subln,
    }
    seq = x_prompt.shape[1]
    t_new = x_sample.shape[1]
    prompt_tables = _rope_tables(jnp.arange(seq, dtype=jnp.int32))
    sample_pos = PAST_LEN + (jnp.arange(FFN_TM, dtype=jnp.int32) % t_new)
    sample_tables = _rope_tables(sample_pos)

    y_prompt, k_prompt, v_prompt = _trunk(x_prompt, prompt_tables, None, weights)
    y_sample, k_sample, v_sample = _trunk(x_sample, sample_tables, (cache_k, cache_v, page_table), weights)
    return (y_prompt, y_sample, k_prompt, v_prompt, k_sample, v_sample)
```

```python
import functools
import math

import jax
import jax.numpy as jnp
from jax import lax
from jax.experimental import pallas as pl
from jax.experimental.pallas import tpu as pltpu

F32 = jnp.float32
BF16 = jnp.bfloat16

D_MODEL = 1024
HEAD_DIM = 64
D_FF = 2816
DEPTH = 4
PAGE_SIZE = 128
PAST_LEN = 2048
N_PAGES = PAST_LEN // PAGE_SIZE
MOBA_BLOCK = 256
MOBA_TOPK = 3
ROPE_THETA = 10000.0
NORM_EPS = 1e-6
SCALE = HEAD_DIM ** -0.5
HALF = D_MODEL // 2
LANES = 128
NEG = -1e30

VMEM_LIMIT = 56 * 1024 * 1024

FFN_TM = 512
FFN_TF = 1408
ATT_TQ = 256
SB_TK = 128
SB_KB = 256
SB_TQ = 512
DEC_PAGES = 8


def _params(sem):
    return pltpu.CompilerParams(dimension_semantics=sem, vmem_limit_bytes=VMEM_LIMIT)


def _rms(x, g):
    return x * lax.rsqrt(jnp.mean(x * x, axis=-1, keepdims=True) + NORM_EPS) * g


def _dot_nt(a, b):
    return lax.dot_general(a, b, (((1,), (1,)), ((), ())), preferred_element_type=F32)


def _dot(a, b):
    return jnp.dot(a, b, preferred_element_type=F32)


def _iota(shape, dim):
    return lax.broadcasted_iota(jnp.int32, shape, dim)


def _ffn_kernel(x_ref, gpre_ref, gpost_ref, wg_ref, wu_ref, wo_ref, o_ref, xn_ref, acc_ref):
    j = pl.program_id(1)

    @pl.when(j == 0)
    def _():
        xn_ref[...] = _rms(x_ref[...], gpre_ref[...]).astype(BF16)

    xn = xn_ref[...]
    gate = _dot(xn, wg_ref[...])
    up = _dot(xn, wu_ref[...])
    h = (gate * jax.nn.sigmoid(gate) * up).astype(BF16)
    part = _dot(h, wo_ref[...])

    @pl.when(j == 0)
    def _():
        acc_ref[...] = part

    @pl.when(j > 0)
    def _():
        acc_ref[...] += part

    @pl.when(j == pl.num_programs(1) - 1)
    def _():
        o_ref[...] = x_ref[...] + 0.5 * _rms(acc_ref[...], gpost_ref[...])


def _ffn(x, g_pre, g_post, w_in, w_out, layer):
    n = x.shape[0]
    nf = D_FF // FFN_TF
    return pl.pallas_call(
        _ffn_kernel,
        grid=(n // FFN_TM, nf),
        in_specs=[
            pl.BlockSpec((FFN_TM, D_MODEL), lambda i, j: (i, 0)),
            pl.BlockSpec((1, D_MODEL), lambda i, j: (0, 0)),
            pl.BlockSpec((1, D_MODEL), lambda i, j: (0, 0)),
            pl.BlockSpec((None, D_MODEL, FFN_TF), lambda i, j: (layer, 0, j)),
            pl.BlockSpec((None, D_MODEL, FFN_TF), lambda i, j: (layer, 0, j + nf)),
            pl.BlockSpec((None, FFN_TF, D_MODEL), lambda i, j: (layer, j, 0)),
        ],
        out_specs=pl.BlockSpec((FFN_TM, D_MODEL), lambda i, j: (i, 0)),
        out_shape=jax.ShapeDtypeStruct((n, D_MODEL), F32),
        scratch_shapes=[pltpu.VMEM((FFN_TM, D_MODEL), BF16), pltpu.VMEM((FFN_TM, D_MODEL), F32)],
        compiler_params=_params(("parallel", "arbitrary")),
    )(x, g_pre, g_post, w_in, w_in, w_out)


def _inproj_kernel(x_ref, g_ref, w_ref, cos_ref, sina_ref, sinb_ref, q_ref, k_ref, v_ref, *, rotary):
    xn = _rms(x_ref[...], g_ref[...]).astype(BF16)
    outs = (q_ref, k_ref, v_ref)
    for c in range(3):
        y = _dot(xn, w_ref[:, c * D_MODEL:(c + 1) * D_MODEL])
        if rotary and c < 2:
            cos, sina, sinb = cos_ref[...], sina_ref[...], sinb_ref[...]
            for p in range(D_MODEL // LANES):
                yp = y[:, p * LANES:(p + 1) * LANES]
                fwd = pltpu.roll(yp, LANES - HEAD_DIM // 2, 1)
                bwd = pltpu.roll(yp, HEAD_DIM // 2, 1)
                outs[c][:, p * LANES:(p + 1) * LANES] = yp * cos + fwd * sina + bwd * sinb
        else:
            outs[c][...] = y


def _inproj(x, g, w, layer, tables, rotary):
    n = x.shape[0]
    cos, sina, sinb = tables
    nt = cos.shape[0] // FFN_TM
    tab = pl.BlockSpec((FFN_TM, LANES), lambda i: (i % nt, 0))
    row = pl.BlockSpec((FFN_TM, D_MODEL), lambda i: (i, 0))
    return pl.pallas_call(
        functools.partial(_inproj_kernel, rotary=rotary),
        grid=(n // FFN_TM,),
        in_specs=[
            row,
            pl.BlockSpec((1, D_MODEL), lambda i: (0, 0)),
            pl.BlockSpec((None, D_MODEL, 3 * D_MODEL), lambda i: (layer, 0, 0)),
            tab, tab, tab,
        ],
        out_specs=[row, row, row],
        out_shape=[jax.ShapeDtypeStruct((n, D_MODEL), F32)] * 3,
        compiler_params=_params(("parallel",)),
    )(x, g, w, cos, sina, sinb)


def _outproj_kernel(o_ref, x_ref, w_ref, g_ref, y_ref):
    m = _dot(o_ref[...], w_ref[...])
    y_ref[...] = x_ref[...] + _rms(m, g_ref[...])


def _outproj(o, x, w, g, layer):
    n = x.shape[0]
    row = pl.BlockSpec((FFN_TM, D_MODEL), lambda i: (i, 0))
    return pl.pallas_call(
        _outproj_kernel,
        grid=(n // FFN_TM,),
        in_specs=[
            row, row,
            pl.BlockSpec((None, D_MODEL, D_MODEL), lambda i: (layer, 0, 0)),
            pl.BlockSpec((1, D_MODEL), lambda i: (0, 0)),
        ],
        out_specs=row,
        out_shape=jax.ShapeDtypeStruct((n, D_MODEL), F32),
        compiler_params=_params(("parallel",)),
    )(o, x, w, g)


def _stack_heads(x):
    low = _iota(x.shape, 1) < HEAD_DIM
    return jnp.concatenate([jnp.where(low, x, 0.0), jnp.where(low, 0.0, x)], axis=0)


def _softmax_step(s, vb, m_ref, l_ref, acc_ref):
    m_old = m_ref[...]
    m_new = jnp.maximum(m_old, jnp.max(s, axis=1, keepdims=True))
    alpha = jnp.exp(m_old - m_new)
    p = jnp.exp(s - m_new)
    l_ref[...] = alpha * l_ref[...] + jnp.sum(p, axis=1, keepdims=True)
    acc_ref[...] = alpha * acc_ref[...] + _dot(p.astype(BF16), vb)
    m_ref[...] = m_new


def _softmax_step_t(s, m_ref, l_ref):
    m_old = m_ref[...]
    m_new = jnp.maximum(m_old, jnp.max(s, axis=0, keepdims=True))
    alpha = jnp.exp(m_old - m_new)
    p = jnp.exp(s - m_new)
    l_ref[...] = alpha * l_ref[...] + jnp.sum(p, axis=0, keepdims=True)
    m_ref[...] = m_new
    return alpha, p.astype(BF16)


def _lambda_value(lam_ref, lam_init):
    lv = lam_ref[...]
    a = jnp.sum(lv[0:1] * lv[1:2], axis=1, keepdims=True)
    b = jnp.sum(lv[2:3] * lv[3:4], axis=1, keepdims=True)
    return jnp.exp(a) - jnp.exp(b) + lam_init


def _sb_weights():
    shape = (2 * SB_TK, 2 * SB_TK)
    j = _iota(shape, 0) & (SB_TK - 1)
    s = _iota(shape, 1)
    return jnp.where((s >= SB_TK) | (j >= s), 1.0, 0.0).astype(BF16)


def _sb_suffix(sp, w):
    hi = sp.astype(BF16)
    lo = (sp - hi.astype(F32)).astype(BF16)
    r = _dot(jnp.concatenate([hi, lo], axis=1), w)
    return r[:, :SB_TK], r[:, SB_TK:]


def _softplus(z):
    return jnp.maximum(z, 0.0) + jnp.log(1.0 + jnp.exp(-jnp.abs(z)))


def _rank_select_t(gate, eligible):
    blk = _iota(gate.shape, 0)
    g = jnp.where(eligible, gate, -jnp.inf)
    cnt = jnp.zeros(gate.shape, jnp.int32)
    for jp in range(gate.shape[0]):
        row = g[jp:jp + 1, :]
        beats = (row > g) | ((row == g) & (jp < blk))
        cnt = cnt + jnp.where(beats, 1, 0)
    return (cnt < MOBA_TOPK) & eligible


def _rank_select(gate, n_blocks, eligible):
    lane = _iota(gate.shape, 1)
    g = jnp.where(eligible, gate, -jnp.inf)
    cnt = jnp.zeros(gate.shape, jnp.int32)
    for jp in range(n_blocks):
        col = g[:, jp:jp + 1]
        beats = (col > g) | ((col == g) & (jp < lane))
        cnt = cnt + jnp.where(beats, 1, 0)
    return (cnt < MOBA_TOPK) & eligible


def _store_v_transposed(v_ref, vt_ref):
    tk = vt_ref.shape[2]
    for c in range(vt_ref.shape[0]):
        vt_ref[c] = v_ref[0, c * tk:(c + 1) * tk, :].T.astype(BF16)


def _causal_t(shape, tk):
    return (_iota(shape, 0) & (tk - 1)) <= _iota(shape, 1)


def _pipelined_blocks(i, scores, consume, st_ref):
    st_ref[...] = scores(i, True)

    def before(j):
        return jnp.where(j == 0, i, j - 1)

    def pair(n, c):
        j = 2 * n
        first = scores(j, False)
        consume(st_ref[...], before(j))
        st_ref[...] = scores(j + 1, False)
        consume(first, j)
        return c

    def single(n, c):
        j = i - 1
        pending = st_ref[...]
        st_ref[...] = scores(j, False)
        consume(pending, before(j))
        return c

    lax.fori_loop(0, i >> 1, pair, 0)
    lax.fori_loop(0, i & 1, single, 0)
    consume(st_ref[...], jnp.maximum(i - 1, 0))


def _diff_prompt_kernel(lam_ref, sg_ref, q_ref, k_ref, v_ref, o_ref, vt_ref, st_ref, m_ref, l_ref, acc_ref, *,
                        lam_init):
    i = pl.program_id(2)
    tq = ATT_TQ

    @pl.when(i == 0)
    def _():
        _store_v_transposed(v_ref, vt_ref)

    qs = (q_ref[0] * SCALE).astype(BF16)
    m_ref[...] = jnp.full(m_ref.shape, NEG, F32)
    l_ref[...] = jnp.zeros(l_ref.shape, F32)
    acc_ref[...] = jnp.zeros(acc_ref.shape, F32)

    def scores(j, masked):
        kb = k_ref[0, pl.ds(pl.multiple_of(j * tq, tq), tq), :].astype(BF16)
        st = _dot_nt(_stack_heads(kb), qs)
        return jnp.where(_causal_t(st.shape, tq), st, NEG) if masked else st

    def consume(st, j):
        vt = vt_ref[j]
        for mp in range(2):
            alpha, p = _softmax_step_t(st[mp * tq:(mp + 1) * tq], m_ref.at[mp], l_ref.at[mp])
            acc_ref[mp] = alpha * acc_ref[mp] + _dot(vt, p)

    _pipelined_blocks(i, scores, consume, st_ref)

    a = acc_ref[0] / l_ref[0] - _lambda_value(lam_ref, lam_init) * (acc_ref[1] / l_ref[1])
    y = a * lax.rsqrt(jnp.mean(a * a, axis=0, keepdims=True) + NORM_EPS) * sg_ref[...] * (1.0 - lam_init)
    o_ref[0] = y.T.astype(o_ref.dtype)


def _moba_prompt_kernel(q_ref, k_ref, v_ref, o_ref, vt_ref, st_ref, kbar_ref, bias_ref, m_ref, l_ref, acc_ref):
    i = pl.program_id(2)
    tq = ATT_TQ
    n_blocks = k_ref.shape[1] // MOBA_BLOCK

    @pl.when(i == 0)
    def _():
        _store_v_transposed(v_ref, vt_ref)
        low = _iota((1, LANES), 1) < HEAD_DIM
        for jb in range(n_blocks):
            blk = k_ref[0, jb * MOBA_BLOCK:(jb + 1) * MOBA_BLOCK, :]
            kbar = jnp.sum(blk, axis=0, keepdims=True) * (1.0 / MOBA_BLOCK)
            kbar_ref[0, jb:jb + 1, :] = jnp.where(low, kbar, 0.0)
            kbar_ref[1, jb:jb + 1, :] = jnp.where(low, 0.0, kbar)

    q = q_ref[0]
    for h in range(2):
        gate = lax.dot_general(kbar_ref[h], q, (((1,), (1,)), ((), ())),
                               precision=lax.Precision.HIGHEST, preferred_element_type=F32)
        sel = _rank_select_t(gate, _iota(gate.shape, 0) < i)
        bias_ref[h] = jnp.where(sel, 0.0, NEG)
    qs = (q * SCALE).astype(BF16)

    m_ref[...] = jnp.full(m_ref.shape, NEG, F32)
    l_ref[...] = jnp.zeros(l_ref.shape, F32)
    acc_ref[...] = jnp.zeros(acc_ref.shape, F32)
    head_a = _iota((LANES, tq), 0) < HEAD_DIM

    def scores(j, diagonal):
        kb = k_ref[0, pl.ds(pl.multiple_of(j * tq, tq), tq), :].astype(BF16)
        st = _dot_nt(_stack_heads(kb), qs)
        if diagonal:
            return jnp.where(_causal_t(st.shape, tq), st, NEG)
        return jnp.concatenate([st[h * tq:(h + 1) * tq] + bias_ref[h, pl.ds(j, 1), :] for h in range(2)], axis=0)

    def consume(st, j):
        alphas, ps = [], []
        for h in range(2):
            alpha, p = _softmax_step_t(st[h * tq:(h + 1) * tq], m_ref.at[h], l_ref.at[h])
            alphas.append(alpha)
            ps.append(p)
        vt = vt_ref[j]
        keep = _iota(vt.shape, 0) < HEAD_DIM
        vbd = jnp.concatenate([jnp.where(keep, vt, 0.0), jnp.where(keep, 0.0, vt)], axis=1)
        alpha = jnp.where(head_a, alphas[0], alphas[1])
        acc_ref[...] = alpha * acc_ref[...] + _dot(vbd, jnp.concatenate(ps, axis=0))

    _pipelined_blocks(i, scores, consume, st_ref)

    o = acc_ref[...] / jnp.where(head_a, l_ref[0], l_ref[1])
    o_ref[0] = o.T.astype(o_ref.dtype)


def _sb_scores(qs, kb, w, strict):
    z = _dot_nt(qs, _stack_heads(kb))
    if strict is not None:
        z = jnp.where(strict, z, NEG)
    return _sb_reduce(z, w)


def _sb_reduce(z, w):
    sp = _softplus(z)
    parts = [_sb_suffix(sp[:, g * SB_TK:(g + 1) * SB_TK], w) for g in range(z.shape[1] // SB_TK)]
    suffix = jnp.concatenate([p[0] for p in parts], axis=1)
    rowsum = jnp.concatenate([p[1] for p in parts], axis=1)
    return z - suffix, rowsum


def _sb_apply(t, rowsum, vb, carry_ref, acc_ref):
    groups = t.shape[1] // (2 * SB_TK)
    carry = carry_ref[...]
    weights = [None] * (2 * groups)
    carries = []
    for h in range(2):
        c = carry[:, h * SB_TK:(h + 1) * SB_TK]
        for u in reversed(range(groups)):
            g = h * groups + u
            weights[g] = jnp.exp(t[:, g * SB_TK:(g + 1) * SB_TK] - c).astype(BF16)
            c = c + rowsum[:, g * SB_TK:(g + 1) * SB_TK]
        carries.append(c)
    acc_ref[...] += _dot(jnp.concatenate(weights, axis=1), _stack_heads(vb))
    carry_ref[...] = jnp.concatenate(carries, axis=1)


def _sb_prompt_kernel(q_ref, k_ref, v_ref, o_ref, t_ref, rs_ref, carry_ref, acc_ref):
    i = pl.program_id(2)
    tq, tk = SB_TQ, SB_KB
    sub = tq // tk
    qs = (q_ref[0] * SCALE).astype(BF16)
    w = _sb_weights()
    carry_ref[...] = jnp.zeros(carry_ref.shape, F32)
    acc_ref[...] = jnp.zeros(acc_ref.shape, F32)

    def scores(j, strict):
        kb = k_ref[0, pl.ds(pl.multiple_of(j * tk, tk), tk), :].astype(BF16)
        return _sb_scores(qs, kb, w, strict)

    def apply(t, rowsum, j):
        vb = v_ref[0, pl.ds(pl.multiple_of(j * tk, tk), tk), :].astype(BF16)
        _sb_apply(t, rowsum, vb, carry_ref, acc_ref)

    shape = (tq, 2 * tk)
    pending = None
    for u in reversed(range(sub)):
        kpos = u * tk + (_iota(shape, 1) & (tk - 1))
        fresh = scores(i * sub + u, kpos < _iota(shape, 0))
        if pending is not None:
            apply(*pending, i * sub + u + 1)
        pending = fresh
    t_ref[...], rs_ref[...] = pending

    def raw(j):
        kb = k_ref[0, pl.ds(pl.multiple_of(j * tk, tk), tk), :].astype(BF16)
        return _dot_nt(qs, _stack_heads(kb))

    def body(n, c):
        j = (i - n) * sub - 1
        z = raw(j)
        apply(t_ref[...], rs_ref[...], j + 1)
        for u in range(1, sub):
            t, rowsum = _sb_reduce(z, w)
            z = raw(j - u)
            apply(t, rowsum, j - u + 1)
        t_ref[...], rs_ref[...] = _sb_reduce(z, w)
        return c

    lax.fori_loop(0, i, body, 0)
    apply(t_ref[...], rs_ref[...], 0)
    o_ref[0] = acc_ref[...].astype(o_ref.dtype)


def _prompt_attention_call(kernel, q, k, v, col0, n_col, extra_in, extra_specs, scratch, tq=ATT_TQ):
    b, t, _ = q.shape
    qspec = pl.BlockSpec((1, tq, LANES), lambda bi, c, i: (bi, i, col0 + c))
    kvspec = pl.BlockSpec((1, t, LANES), lambda bi, c, i: (bi, 0, col0 + c))
    return pl.pallas_call(
        kernel,
        grid=(b, n_col, t // tq),
        in_specs=list(extra_specs) + [qspec, kvspec, kvspec],
        out_specs=pl.BlockSpec((1, tq, LANES), lambda bi, c, i: (bi, i, c)),
        out_shape=jax.ShapeDtypeStruct((b, t, n_col * LANES), BF16),
        scratch_shapes=scratch,
        compiler_params=_params(("parallel", "parallel", "arbitrary")),
    )(*extra_in, q, k, v)


def _prompt_even_attention(q, k, v, lam_vecs, subln, lam_init):
    t = q.shape[1]
    tq = ATT_TQ
    whole = lambda shape: pl.BlockSpec(shape, lambda bi, c, i: (0, 0))
    vt = [pltpu.VMEM((t // tq, LANES, tq), BF16), pltpu.VMEM((2 * tq, tq), F32)]
    stats = [pltpu.VMEM((2, 1, tq), F32), pltpu.VMEM((2, 1, tq), F32)]
    sg = jnp.broadcast_to(subln.reshape(2 * HEAD_DIM, 1), (2 * HEAD_DIM, tq))
    o_a = _prompt_attention_call(
        functools.partial(_diff_prompt_kernel, lam_init=lam_init), q, k, v, 0, HALF // LANES,
        (lam_vecs, sg), (whole(lam_vecs.shape), whole(sg.shape)),
        vt + stats + [pltpu.VMEM((2, LANES, tq), F32)])
    n_blocks = t // MOBA_BLOCK
    o_b = _prompt_attention_call(
        _moba_prompt_kernel, q, k, v, HALF // LANES, HALF // LANES, (), (),
        vt + [pltpu.VMEM((2, n_blocks, LANES), F32), pltpu.VMEM((2, n_blocks, tq), F32)] + stats
        + [pltpu.VMEM((LANES, tq), F32)])
    return jnp.concatenate([o_a, o_b], axis=-1)


def _prompt_odd_attention(q, k, v):
    return _prompt_attention_call(
        _sb_prompt_kernel, q, k, v, 0, D_MODEL // LANES, (), (),
        [pltpu.VMEM((SB_TQ, 2 * SB_KB), F32)] * 2 + [pltpu.VMEM((SB_TQ, 2 * SB_TK), F32),
                                                     pltpu.VMEM((SB_TQ, LANES), F32)], tq=SB_TQ)


def _block_diag(q, n_heads, width):
    t = q.shape[0]
    tiled = jnp.concatenate([q] * n_heads, axis=0)
    keep = _head_mask(tiled.shape, t, width)
    return jnp.where(keep, tiled, 0.0)


def _head_mask(shape, rows_per_head, width):
    shift_r = rows_per_head.bit_length() - 1
    shift_c = width.bit_length() - 1
    return (_iota(shape, 0) >> shift_r) == (_iota(shape, 1) >> shift_c)


def _fold_heads(x, t):
    out = x[0:t]
    for h in range(1, x.shape[0] // t):
        out = out + x[h * t:(h + 1) * t]
    return out


def _pad_rows(x, rows):
    return jnp.concatenate([x, jnp.zeros((rows - x.shape[0], x.shape[1]), x.dtype)], axis=0)


def _even_decode_kernel(pt_ref, lam_ref, sg_ref, q_ref, kn_ref, vn_ref, *rest, lam_init, t_new):
    del pt_ref
    ck_refs, cv_refs = rest[:DEC_PAGES], rest[DEC_PAGES:2 * DEC_PAGES]
    (o_ref, qd_ref, qm_ref, qmf_ref, md_ref, ld_ref, accd_ref,
     mb_ref, lb_ref, accb_ref, gate_ref) = rest[2 * DEC_PAGES:]
    p = pl.program_id(1)
    n_steps = N_PAGES // DEC_PAGES
    n_maps = HALF // HEAD_DIM
    rows = n_maps * t_new
    blocks_per_step = DEC_PAGES * PAGE_SIZE // MOBA_BLOCK
    n_blocks = PAST_LEN // MOBA_BLOCK

    @pl.when(p == 0)
    def _():
        q = q_ref[0]
        qd_ref[...] = (_block_diag(q[:, :HALF], n_maps, HEAD_DIM) * SCALE).astype(BF16)
        qmf = _block_diag(q[:, HALF:], n_maps, HEAD_DIM)
        qmf_ref[...] = qmf
        qm_ref[...] = (qmf * SCALE).astype(BF16)
        md_ref[...] = jnp.full(md_ref.shape, NEG, F32)
        ld_ref[...] = jnp.zeros(ld_ref.shape, F32)
        accd_ref[...] = jnp.zeros(accd_ref.shape, F32)
        gate_ref[...] = jnp.zeros(gate_ref.shape, F32)

    @pl.when(p < n_steps)
    def _():
        kcat = jnp.concatenate([r[...] for r in ck_refs], axis=0)
        vcat = jnp.concatenate([r[...] for r in cv_refs], axis=0)
        s = _dot_nt(qd_ref[...], kcat[:, :HALF].astype(BF16))
        _softmax_step(s, vcat[:, :HALF].astype(BF16), md_ref, ld_ref, accd_ref)
        km = kcat[:, HALF:]
        vm = vcat[:, HALF:].astype(BF16)
        s = _dot_nt(qm_ref[...], km.astype(BF16))
        gate = gate_ref[...]
        for u in range(blocks_per_step):
            jb = p * blocks_per_step + u
            sb = s[:, u * MOBA_BLOCK:(u + 1) * MOBA_BLOCK]
            m = jnp.max(sb, axis=1, keepdims=True)
            pe = jnp.exp(sb - m)
            mb_ref[jb] = m
            lb_ref[jb] = jnp.sum(pe, axis=1, keepdims=True)
            accb_ref[jb] = _dot(pe.astype(BF16), vm[u * MOBA_BLOCK:(u + 1) * MOBA_BLOCK])
            kbar = jnp.sum(km[u * MOBA_BLOCK:(u + 1) * MOBA_BLOCK], axis=0, keepdims=True) * (1.0 / MOBA_BLOCK)
            g = jnp.sum(qmf_ref[...] * kbar, axis=1, keepdims=True)
            gate = jnp.where(_iota(gate.shape, 1) == jb, g, gate)
        gate_ref[...] = gate

    @pl.when(p == n_steps)
    def _():
        kpad = _pad_rows(kn_ref[0], PAGE_SIZE).astype(BF16)
        vpad = _pad_rows(vn_ref[0], PAGE_SIZE).astype(BF16)
        shape = (rows, PAGE_SIZE)
        causal = _iota(shape, 1) <= (_iota(shape, 0) & (t_new - 1))

        s = jnp.where(causal, _dot_nt(qd_ref[...], kpad[:, :HALF]), NEG)
        _softmax_step(s, vpad[:, :HALF], md_ref, ld_ref, accd_ref)
        od = accd_ref[...] / ld_ref[...]
        map_id = _iota(od.shape, 0) >> (t_new.bit_length() - 1)
        coef = jnp.where((map_id & 1) == 0, 1.0, -_lambda_value(lam_ref, lam_init))
        own = (map_id >> 1) == (_iota(od.shape, 1) >> ((2 * HEAD_DIM).bit_length() - 1))
        oa = _fold_heads(jnp.where(own, od * coef, 0.0), t_new)
        sg = sg_ref[...]
        for h in range(HALF // LANES):
            seg = oa[:, h * LANES:(h + 1) * LANES]
            o_ref[0, :, h * LANES:(h + 1) * LANES] = _rms(seg, sg) * (1.0 - lam_init)

        s = jnp.where(causal, _dot_nt(qm_ref[...], kpad[:, HALF:]), NEG)
        m_own = jnp.max(s, axis=1, keepdims=True)
        p_own = jnp.exp(s - m_own)
        l_own = jnp.sum(p_own, axis=1, keepdims=True)
        acc_own = _dot(p_own.astype(BF16), vpad[:, HALF:])
        gate = gate_ref[...]
        sel = _rank_select(gate, n_blocks, _iota(gate.shape, 1) < n_blocks)
        m_all = m_own
        for j in range(n_blocks):
            m_all = jnp.maximum(m_all, jnp.where(sel[:, j:j + 1], mb_ref[j], NEG))
        w_own = jnp.exp(m_own - m_all)
        num = w_own * acc_own
        den = w_own * l_own
        for j in range(n_blocks):
            w = jnp.where(sel[:, j:j + 1], jnp.exp(mb_ref[j] - m_all), 0.0)
            num = num + w * accb_ref[j]
            den = den + w * lb_ref[j]
        ob = jnp.where(_head_mask(num.shape, t_new, HEAD_DIM), num / den, 0.0)
        o_ref[0, :, HALF:] = _fold_heads(ob, t_new)


def _odd_decode_kernel(pt_ref, q_ref, kn_ref, vn_ref, *rest, t_new):
    del pt_ref
    ck_refs, cv_refs = rest[:DEC_PAGES], rest[DEC_PAGES:2 * DEC_PAGES]
    o_ref, qs_ref, carry_ref, acc_ref = rest[2 * DEC_PAGES:]
    p = pl.program_id(1)
    n_heads = D_MODEL // HEAD_DIM
    rows = n_heads * t_new

    def page(k_bf, v_bf, strict, w):
        z = _dot_nt(qs_ref[...], k_bf)
        sp = _softplus(z)
        if strict is not None:
            sp = jnp.where(strict, sp, 0.0)
        suffix, rowsum = _sb_suffix(sp, w)
        return z, suffix, rowsum

    @pl.when(p == 0)
    def _():
        qs_ref[...] = (_block_diag(q_ref[0], n_heads, HEAD_DIM) * SCALE).astype(BF16)
        acc_ref[...] = jnp.zeros(acc_ref.shape, F32)
        kpad = _pad_rows(kn_ref[0], PAGE_SIZE).astype(BF16)
        vpad = _pad_rows(vn_ref[0], PAGE_SIZE).astype(BF16)
        shape = (rows, PAGE_SIZE)
        strict = _iota(shape, 1) < (_iota(shape, 0) & (t_new - 1))
        z, suffix, rowsum = page(kpad, vpad, strict, _sb_weights())
        a = jnp.where(strict, jnp.exp(z - suffix), 0.0)
        acc_ref[...] += _dot(a.astype(BF16), vpad)
        carry_ref[...] = rowsum

    @pl.when(p > 0)
    def _():
        w = _sb_weights()
        carry = carry_ref[...]
        weights = []
        for u in range(DEC_PAGES):
            z, suffix, rowsum = page(ck_refs[u][...].astype(BF16), None, None, w)
            weights.append(jnp.exp(z - suffix - carry).astype(BF16))
            carry = carry + rowsum
        carry_ref[...] = carry
        vcat = jnp.concatenate([r[...] for r in cv_refs], axis=0).astype(BF16)
        acc_ref[...] += _dot(jnp.concatenate(weights, axis=1), vcat)

    @pl.when(p == N_PAGES // DEC_PAGES)
    def _():
        acc = acc_ref[...]
        o_ref[0] = _fold_heads(jnp.where(_head_mask(acc.shape, t_new, HEAD_DIM), acc, 0.0), t_new)


def _decode_attention(kernel, q, k_new, v_new, cache_k, cache_v, page_table, layer, page_of, extra_in, scratch):
    b, t_new, _ = q.shape
    new = pl.BlockSpec((1, t_new, D_MODEL), lambda bi, p, pt: (bi, 0, 0))

    def page(u):
        return pl.BlockSpec((None, None, PAGE_SIZE, D_MODEL),
                            lambda bi, p, pt: (layer, pt[bi, page_of(p, u)], 0, 0))

    pages = [page(u) for u in range(DEC_PAGES)]
    extra_specs = [pl.BlockSpec(a.shape, lambda bi, p, pt: (0, 0)) for a in extra_in]
    return pl.pallas_call(
        kernel,
        grid_spec=pltpu.PrefetchScalarGridSpec(
            num_scalar_prefetch=1,
            grid=(b, N_PAGES // DEC_PAGES + 1),
            in_specs=extra_specs + [new, new, new] + pages + pages,
            out_specs=new,
            scratch_shapes=scratch,
        ),
        out_shape=jax.ShapeDtypeStruct((b, t_new, D_MODEL), F32),
        compiler_params=_params(("parallel", "arbitrary")),
    )(page_table, *extra_in, q, k_new, v_new, *([cache_k] * DEC_PAGES), *([cache_v] * DEC_PAGES))


def _decode_even_attention(q, k_new, v_new, cache_k, cache_v, page_table, layer, lam_vecs, subln, lam_init):
    t_new = q.shape[1]
    rows = (HALF // HEAD_DIM) * t_new
    n_blocks = PAST_LEN // MOBA_BLOCK
    n_steps = N_PAGES // DEC_PAGES
    scratch = [
        pltpu.VMEM((rows, HALF), BF16), pltpu.VMEM((rows, HALF), BF16), pltpu.VMEM((rows, HALF), F32),
        pltpu.VMEM((rows, 1), F32), pltpu.VMEM((rows, 1), F32), pltpu.VMEM((rows, HALF), F32),
        pltpu.VMEM((n_blocks, rows, 1), F32), pltpu.VMEM((n_blocks, rows, 1), F32),
        pltpu.VMEM((n_blocks, rows, HALF), F32),
        pltpu.VMEM((rows, LANES), F32),
    ]
    page_of = lambda p, u: jnp.minimum(p, n_steps - 1) * DEC_PAGES + u
    return _decode_attention(
        functools.partial(_even_decode_kernel, lam_init=lam_init, t_new=t_new),
        q, k_new, v_new, cache_k, cache_v, page_table, layer, page_of, (lam_vecs, subln), scratch)


def _decode_odd_attention(q, k_new, v_new, cache_k, cache_v, page_table, layer):
    t_new = q.shape[1]
    rows = (D_MODEL // HEAD_DIM) * t_new
    scratch = [pltpu.VMEM((rows, D_MODEL), BF16), pltpu.VMEM((rows, LANES), F32), pltpu.VMEM((rows, D_MODEL), F32)]
    page_of = lambda p, u: N_PAGES - (jnp.maximum(p, 1) - 1) * DEC_PAGES - 1 - u
    return _decode_attention(
        functools.partial(_odd_decode_kernel, t_new=t_new),
        q, k_new, v_new, cache_k, cache_v, page_table, layer, page_of, (), scratch)


def _rope_tables(pos):
    half = HEAD_DIM // 2
    inv_freq = 1.0 / (ROPE_THETA ** (jnp.arange(half, dtype=F32) * (2.0 / HEAD_DIM)))
    ang = pos.astype(F32)[:, None] * inv_freq[None, :]
    cos, sin, zero = jnp.cos(ang), jnp.sin(ang), jnp.zeros_like(ang)
    reps = LANES // HEAD_DIM
    return (jnp.concatenate([cos, cos] * reps, axis=1),
            jnp.concatenate([-sin, zero] * reps, axis=1),
            jnp.concatenate([zero, sin] * reps, axis=1))


def _lambda_init(layer):
    return 0.8 - 0.6 * math.exp(-0.3 * layer)


def _trunk(x3, tables, cache, weights):
    b, t, _ = x3.shape
    n = b * t
    x = x3.reshape(n, D_MODEL)
    gains = weights["norm_gains"]
    new_k, new_v = [], []
    for l in range(DEPTH):
        g = [gains[l, i].reshape(1, D_MODEL) for i in range(6)]
        x = _ffn(x, g[0], g[1], weights["ffn1_w_in"], weights["ffn1_w_out"], l)
        even = l % 2 == 0
        q, k, v = _inproj(x, g[2], weights["mix_w_in"], l, tables, even)
        q3, k3, v3 = (a.reshape(b, t, D_MODEL) for a in (q, k, v))
        if even:
            lam_vecs = weights["diff_lambda"][l // 2]
            subln = weights["diff_subln"][l // 2].reshape(1, 2 * HEAD_DIM)
            if cache is None:
                o = _prompt_even_attention(q3, k3, v3, lam_vecs, subln, _lambda_init(l))
            else:
                o = _decode_even_attention(q3, k3, v3, *cache, l, lam_vecs, subln, _lambda_init(l))
        else:
            if cache is None:
                o = _prompt_odd_attention(q3, k3, v3)
            else:
                o = _decode_odd_attention(q3, k3, v3, *cache, l)
        x = _outproj(o.reshape(n, D_MODEL).astype(BF16), x, weights["mix_w_out"], g[3], l)
        x = _ffn(x, g[4], g[5], weights["ffn2_w_in"], weights["ffn2_w_out"], l)
        new_k.append(k3)
        new_v.append(v3)
    return x.reshape(b, t, D_MODEL), jnp.stack(new_k), jnp.stack(new_v)


def _even_layer_in_proj(w):
    return jnp.concatenate([w[:, c * HALF:(c + 1) * HALF] for c in (0, 3, 1, 4, 2, 5)], axis=1)


def kernel(x_prompt, x_sample, cache_k, cache_v, page_table, norm_gains, ffn1_w_in, ffn1_w_out, mix_w_in,
           mix_w_out, diff_lambda, diff_subln, ffn2_w_in, ffn2_w_out):
    mix_in = jnp.stack([_even_layer_in_proj(mix_w_in[l]) if l % 2 == 0 else mix_w_in[l] for l in range(DEPTH)])
    weights = {
        "norm_gains": norm_gains,
        "ffn1_w_in": ffn1_w_in.astype(BF16), "ffn1_w_out": ffn1_w_out.astype(BF16),
        "ffn2_w_in": ffn2_w_in.astype(BF16), "ffn2_w_out": ffn2_w_out.astype(BF16),
        "mix_w_in": mix_in.astype(BF16), "mix_w_out": mix_w_out.astype(BF16),
        "diff_lambda": diff_lambda, "diff_subln": diff_subln,
    }
    seq = x_prompt.shape[1]
    t_new = x_sample.shape[1]
    prompt_tables = _rope_tables(jnp.arange(seq, dtype=jnp.int32))
    sample_pos = PAST_LEN + (jnp.arange(FFN_TM, dtype=jnp.int32) % t_new)
    sample_tables = _rope_tables(sample_pos)

    y_prompt, k_prompt, v_prompt = _trunk(x_prompt, prompt_tables, None, weights)
    y_sample, k_sample, v_sample = _trunk(x_sample, sample_tables, (cache_k, cache_v, page_table), weights)
    return (y_prompt, y_sample, k_prompt, v_prompt, k_sample, v_sample)
```

```python
import functools
import math

import jax
import jax.numpy as jnp
from jax import lax
from jax.experimental import pallas as pl
from jax.experimental.pallas import tpu as pltpu

F32 = jnp.float32
BF16 = jnp.bfloat16

D_MODEL = 1024
HEAD_DIM = 64
D_FF = 2816
DEPTH = 4
PAGE_SIZE = 128
PAST_LEN = 2048
N_PAGES = PAST_LEN // PAGE_SIZE
MOBA_BLOCK = 256
MOBA_TOPK = 3
ROPE_THETA = 10000.0
NORM_EPS = 1e-6
SCALE = HEAD_DIM ** -0.5
HALF = D_MODEL // 2
LANES = 128
NEG = -1e30

VMEM_LIMIT = 56 * 1024 * 1024

FFN_TM = 512
FFN_TF = 1408
ATT_KB = MOBA_BLOCK
ATT_TQ = 2 * ATT_KB
SB_TK = 128
SB_KB = 256
SB_TQ = 512
DEC_PAGES = 8


def _params(sem):
    return pltpu.CompilerParams(dimension_semantics=sem, vmem_limit_bytes=VMEM_LIMIT)


def _rms(x, g):
    return x * lax.rsqrt(jnp.mean(x * x, axis=-1, keepdims=True) + NORM_EPS) * g


def _dot_nt(a, b):
    return lax.dot_general(a, b, (((1,), (1,)), ((), ())), preferred_element_type=F32)


def _dot(a, b):
    return jnp.dot(a, b, preferred_element_type=F32)


def _iota(shape, dim):
    return lax.broadcasted_iota(jnp.int32, shape, dim)


def _ffn_kernel(x_ref, gpre_ref, gpost_ref, wg_ref, wu_ref, wo_ref, o_ref, xn_ref, acc_ref):
    j = pl.program_id(1)

    @pl.when(j == 0)
    def _():
        xn_ref[...] = _rms(x_ref[...], gpre_ref[...]).astype(BF16)

    xn = xn_ref[...]
    gate = _dot(xn, wg_ref[...])
    up = _dot(xn, wu_ref[...])
    h = (gate * jax.nn.sigmoid(gate) * up).astype(BF16)
    part = _dot(h, wo_ref[...])

    @pl.when(j == 0)
    def _():
        acc_ref[...] = part

    @pl.when(j > 0)
    def _():
        acc_ref[...] += part

    @pl.when(j == pl.num_programs(1) - 1)
    def _():
        o_ref[...] = x_ref[...] + 0.5 * _rms(acc_ref[...], gpost_ref[...])


def _ffn(x, g_pre, g_post, w_in, w_out, layer):
    n = x.shape[0]
    nf = D_FF // FFN_TF
    return pl.pallas_call(
        _ffn_kernel,
        grid=(n // FFN_TM, nf),
        in_specs=[
            pl.BlockSpec((FFN_TM, D_MODEL), lambda i, j: (i, 0)),
            pl.BlockSpec((1, D_MODEL), lambda i, j: (0, 0)),
            pl.BlockSpec((1, D_MODEL), lambda i, j: (0, 0)),
            pl.BlockSpec((None, D_MODEL, FFN_TF), lambda i, j: (layer, 0, j)),
            pl.BlockSpec((None, D_MODEL, FFN_TF), lambda i, j: (layer, 0, j + nf)),
            pl.BlockSpec((None, FFN_TF, D_MODEL), lambda i, j: (layer, j, 0)),
        ],
        out_specs=pl.BlockSpec((FFN_TM, D_MODEL), lambda i, j: (i, 0)),
        out_shape=jax.ShapeDtypeStruct((n, D_MODEL), F32),
        scratch_shapes=[pltpu.VMEM((FFN_TM, D_MODEL), BF16), pltpu.VMEM((FFN_TM, D_MODEL), F32)],
        compiler_params=_params(("parallel", "arbitrary")),
    )(x, g_pre, g_post, w_in, w_in, w_out)


def _inproj_kernel(x_ref, g_ref, w_ref, cos_ref, sina_ref, sinb_ref, q_ref, k_ref, v_ref, *, rotary):
    xn = _rms(x_ref[...], g_ref[...]).astype(BF16)
    outs = (q_ref, k_ref, v_ref)
    for c in range(3):
        y = _dot(xn, w_ref[:, c * D_MODEL:(c + 1) * D_MODEL])
        if rotary and c < 2:
            cos, sina, sinb = cos_ref[...], sina_ref[...], sinb_ref[...]
            for p in range(D_MODEL // LANES):
                yp = y[:, p * LANES:(p + 1) * LANES]
                fwd = pltpu.roll(yp, LANES - HEAD_DIM // 2, 1)
                bwd = pltpu.roll(yp, HEAD_DIM // 2, 1)
                outs[c][:, p * LANES:(p + 1) * LANES] = yp * cos + fwd * sina + bwd * sinb
        else:
            outs[c][...] = y


def _inproj(x, g, w, layer, tables, rotary):
    n = x.shape[0]
    cos, sina, sinb = tables
    nt = cos.shape[0] // FFN_TM
    tab = pl.BlockSpec((FFN_TM, LANES), lambda i: (i % nt, 0))
    row = pl.BlockSpec((FFN_TM, D_MODEL), lambda i: (i, 0))
    return pl.pallas_call(
        functools.partial(_inproj_kernel, rotary=rotary),
        grid=(n // FFN_TM,),
        in_specs=[
            row,
            pl.BlockSpec((1, D_MODEL), lambda i: (0, 0)),
            pl.BlockSpec((None, D_MODEL, 3 * D_MODEL), lambda i: (layer, 0, 0)),
            tab, tab, tab,
        ],
        out_specs=[row, row, row],
        out_shape=[jax.ShapeDtypeStruct((n, D_MODEL), F32)] * 3,
        compiler_params=_params(("parallel",)),
    )(x, g, w, cos, sina, sinb)


def _outproj_kernel(o_ref, x_ref, w_ref, g_ref, y_ref):
    m = _dot(o_ref[...], w_ref[...])
    y_ref[...] = x_ref[...] + _rms(m, g_ref[...])


def _outproj(o, x, w, g, layer):
    n = x.shape[0]
    row = pl.BlockSpec((FFN_TM, D_MODEL), lambda i: (i, 0))
    return pl.pallas_call(
        _outproj_kernel,
        grid=(n // FFN_TM,),
        in_specs=[
            row, row,
            pl.BlockSpec((None, D_MODEL, D_MODEL), lambda i: (layer, 0, 0)),
            pl.BlockSpec((1, D_MODEL), lambda i: (0, 0)),
        ],
        out_specs=row,
        out_shape=jax.ShapeDtypeStruct((n, D_MODEL), F32),
        compiler_params=_params(("parallel",)),
    )(o, x, w, g)


def _stack_heads(x):
    low = _iota(x.shape, 1) < HEAD_DIM
    return jnp.concatenate([jnp.where(low, x, 0.0), jnp.where(low, 0.0, x)], axis=0)


def _softmax_step(s, vb, m_ref, l_ref, acc_ref):
    m_old = m_ref[...]
    m_new = jnp.maximum(m_old, jnp.max(s, axis=1, keepdims=True))
    alpha = jnp.exp(m_old - m_new)
    p = jnp.exp(s - m_new)
    l_ref[...] = alpha * l_ref[...] + jnp.sum(p, axis=1, keepdims=True)
    acc_ref[...] = alpha * acc_ref[...] + _dot(p.astype(BF16), vb)
    m_ref[...] = m_new


def _softmax_step_t(s, m_ref, l_ref):
    m_old = m_ref[...]
    m_new = jnp.maximum(m_old, jnp.max(s, axis=0, keepdims=True))
    alpha = jnp.exp(m_old - m_new)
    p = jnp.exp(s - m_new)
    l_ref[...] = alpha * l_ref[...] + jnp.sum(p, axis=0, keepdims=True)
    m_ref[...] = m_new
    return alpha, p.astype(BF16)


def _lambda_value(lam_ref, lam_init):
    lv = lam_ref[...]
    a = jnp.sum(lv[0:1] * lv[1:2], axis=1, keepdims=True)
    b = jnp.sum(lv[2:3] * lv[3:4], axis=1, keepdims=True)
    return jnp.exp(a) - jnp.exp(b) + lam_init


def _sb_weights():
    shape = (2 * SB_TK, 2 * SB_TK)
    j = _iota(shape, 0) & (SB_TK - 1)
    s = _iota(shape, 1)
    return jnp.where((s >= SB_TK) | (j >= s), 1.0, 0.0).astype(BF16)


def _sb_suffix(sp, w):
    hi = sp.astype(BF16)
    lo = (sp - hi.astype(F32)).astype(BF16)
    r = _dot(jnp.concatenate([hi, lo], axis=1), w)
    return r[:, :SB_TK], r[:, SB_TK:]


def _softplus(z):
    return jnp.maximum(z, 0.0) + jnp.log(1.0 + jnp.exp(-jnp.abs(z)))


def _rank_select_t(gate, eligible):
    blk = _iota(gate.shape, 0)
    g = jnp.where(eligible, gate, -jnp.inf)
    cnt = jnp.zeros(gate.shape, jnp.int32)
    for jp in range(gate.shape[0]):
        row = g[jp:jp + 1, :]
        beats = (row > g) | ((row == g) & (jp < blk))
        cnt = cnt + jnp.where(beats, 1, 0)
    return (cnt < MOBA_TOPK) & eligible


def _rank_select(gate, n_blocks, eligible):
    lane = _iota(gate.shape, 1)
    g = jnp.where(eligible, gate, -jnp.inf)
    cnt = jnp.zeros(gate.shape, jnp.int32)
    for jp in range(n_blocks):
        col = g[:, jp:jp + 1]
        beats = (col > g) | ((col == g) & (jp < lane))
        cnt = cnt + jnp.where(beats, 1, 0)
    return (cnt < MOBA_TOPK) & eligible


def _store_v_transposed(v_ref, vt_ref):
    tk = vt_ref.shape[2]
    for c in range(vt_ref.shape[0]):
        vt_ref[c] = v_ref[0, c * tk:(c + 1) * tk, :].T.astype(BF16)


def _own_causal_t(shape, u):
    return (u * ATT_KB + (_iota(shape, 0) & (ATT_KB - 1))) <= _iota(shape, 1)


def _pipelined_blocks(i, scores, consume, st_ref):
    st_ref[...] = scores(2 * i + 1, 1)
    own = scores(2 * i, 0)
    consume(st_ref[...], 2 * i + 1)
    st_ref[...] = own

    def pair(n, c):
        j = 2 * n
        first = scores(j, None)
        consume(st_ref[...], jnp.where(j == 0, 2 * i, j - 1))
        st_ref[...] = scores(j + 1, None)
        consume(first, j)
        return c

    lax.fori_loop(0, i, pair, 0)
    consume(st_ref[...], jnp.maximum(2 * i - 1, 0))


def _diff_prompt_kernel(lam_ref, sg_ref, q_ref, k_ref, v_ref, o_ref, vt_ref, st_ref, m_ref, l_ref, acc_ref, *,
                        lam_init):
    i = pl.program_id(2)
    tk = ATT_KB

    @pl.when(i == 0)
    def _():
        _store_v_transposed(v_ref, vt_ref)

    qs = (q_ref[0] * SCALE).astype(BF16)
    m_ref[...] = jnp.full(m_ref.shape, NEG, F32)
    l_ref[...] = jnp.zeros(l_ref.shape, F32)
    acc_ref[...] = jnp.zeros(acc_ref.shape, F32)

    def scores(j, own):
        kb = k_ref[0, pl.ds(pl.multiple_of(j * tk, tk), tk), :].astype(BF16)
        st = _dot_nt(_stack_heads(kb), qs)
        return st if own is None else jnp.where(_own_causal_t(st.shape, own), st, NEG)

    def consume(st, j):
        vt = vt_ref[j]
        for mp in range(2):
            alpha, p = _softmax_step_t(st[mp * tk:(mp + 1) * tk], m_ref.at[mp], l_ref.at[mp])
            acc_ref[mp] = alpha * acc_ref[mp] + _dot(vt, p)

    _pipelined_blocks(i, scores, consume, st_ref)

    a = acc_ref[0] / l_ref[0] - _lambda_value(lam_ref, lam_init) * (acc_ref[1] / l_ref[1])
    y = a * lax.rsqrt(jnp.mean(a * a, axis=0, keepdims=True) + NORM_EPS) * sg_ref[...] * (1.0 - lam_init)
    o_ref[0] = y.T.astype(o_ref.dtype)


def _moba_prompt_kernel(q_ref, k_ref, v_ref, o_ref, vt_ref, st_ref, kbar_ref, bias_ref, m_ref, l_ref, acc_ref):
    i = pl.program_id(2)
    tq, tk = ATT_TQ, ATT_KB
    n_blocks = k_ref.shape[1] // tk

    @pl.when(i == 0)
    def _():
        _store_v_transposed(v_ref, vt_ref)
        low = _iota((1, LANES), 1) < HEAD_DIM
        for jb in range(n_blocks):
            blk = k_ref[0, jb * tk:(jb + 1) * tk, :]
            kbar = jnp.sum(blk, axis=0, keepdims=True) * (1.0 / tk)
            kbar_ref[0, jb:jb + 1, :] = jnp.where(low, kbar, 0.0)
            kbar_ref[1, jb:jb + 1, :] = jnp.where(low, 0.0, kbar)

    q = q_ref[0]
    own_block = _iota((n_blocks, tq), 1) >> (tk.bit_length() - 1)
    for h in range(2):
        gate = lax.dot_general(kbar_ref[h], q, (((1,), (1,)), ((), ())),
                               precision=lax.Precision.HIGHEST, preferred_element_type=F32)
        sel = _rank_select_t(gate, _iota(gate.shape, 0) < 2 * i + own_block)
        bias_ref[h] = jnp.where(sel, 0.0, NEG)
    qs = (q * SCALE).astype(BF16)

    m_ref[...] = jnp.full(m_ref.shape, NEG, F32)
    l_ref[...] = jnp.zeros(l_ref.shape, F32)
    acc_ref[...] = jnp.zeros(acc_ref.shape, F32)
    head_a = _iota((LANES, tq), 0) < HEAD_DIM

    def scores(j, own):
        kb = k_ref[0, pl.ds(pl.multiple_of(j * tk, tk), tk), :].astype(BF16)
        st = _dot_nt(_stack_heads(kb), qs)
        biased = jnp.concatenate([st[h * tk:(h + 1) * tk] + bias_ref[h, pl.ds(j, 1), :] for h in range(2)], axis=0)
        if own is None:
            return biased
        inside = (_iota(st.shape, 1) >> (tk.bit_length() - 1)) == own
        return jnp.where(inside, jnp.where(_own_causal_t(st.shape, own), st, NEG), biased)

    def consume(st, j):
        alphas, ps = [], []
        for h in range(2):
            alpha, p = _softmax_step_t(st[h * tk:(h + 1) * tk], m_ref.at[h], l_ref.at[h])
            alphas.append(alpha)
            ps.append(p)
        vt = vt_ref[j]
        keep = _iota(vt.shape, 0) < HEAD_DIM
        vbd = jnp.concatenate([jnp.where(keep, vt, 0.0), jnp.where(keep, 0.0, vt)], axis=1)
        alpha = jnp.where(head_a, alphas[0], alphas[1])
        acc_ref[...] = alpha * acc_ref[...] + _dot(vbd, jnp.concatenate(ps, axis=0))

    _pipelined_blocks(i, scores, consume, st_ref)

    o = acc_ref[...] / jnp.where(head_a, l_ref[0], l_ref[1])
    o_ref[0] = o.T.astype(o_ref.dtype)


def _sb_scores(qs, kb, w, strict):
    z = _dot_nt(qs, _stack_heads(kb))
    if strict is not None:
        z = jnp.where(strict, z, NEG)
    return _sb_reduce(z, w)


def _sb_reduce(z, w):
    sp = _softplus(z)
    parts = [_sb_suffix(sp[:, g * SB_TK:(g + 1) * SB_TK], w) for g in range(z.shape[1] // SB_TK)]
    suffix = jnp.concatenate([p[0] for p in parts], axis=1)
    rowsum = jnp.concatenate([p[1] for p in parts], axis=1)
    return z - suffix, rowsum


def _sb_apply(t, rowsum, vb, carry_ref, acc_ref):
    groups = t.shape[1] // (2 * SB_TK)
    carry = carry_ref[...]
    weights = [None] * (2 * groups)
    carries = []
    for h in range(2):
        c = carry[:, h * SB_TK:(h + 1) * SB_TK]
        for u in reversed(range(groups)):
            g = h * groups + u
            weights[g] = jnp.exp(t[:, g * SB_TK:(g + 1) * SB_TK] - c).astype(BF16)
            c = c + rowsum[:, g * SB_TK:(g + 1) * SB_TK]
        carries.append(c)
    acc_ref[...] += _dot(jnp.concatenate(weights, axis=1), _stack_heads(vb))
    carry_ref[...] = jnp.concatenate(carries, axis=1)


def _sb_prompt_kernel(q_ref, k_ref, v_ref, o_ref, t_ref, rs_ref, carry_ref, acc_ref):
    i = pl.program_id(2)
    tq, tk = SB_TQ, SB_KB
    sub = tq // tk
    qs = (q_ref[0] * SCALE).astype(BF16)
    w = _sb_weights()
    carry_ref[...] = jnp.zeros(carry_ref.shape, F32)
    acc_ref[...] = jnp.zeros(acc_ref.shape, F32)

    def scores(j, strict):
        kb = k_ref[0, pl.ds(pl.multiple_of(j * tk, tk), tk), :].astype(BF16)
        return _sb_scores(qs, kb, w, strict)

    def apply(t, rowsum, j):
        vb = v_ref[0, pl.ds(pl.multiple_of(j * tk, tk), tk), :].astype(BF16)
        _sb_apply(t, rowsum, vb, carry_ref, acc_ref)

    shape = (tq, 2 * tk)
    pending = None
    for u in reversed(range(sub)):
        kpos = u * tk + (_iota(shape, 1) & (tk - 1))
        fresh = scores(i * sub + u, kpos < _iota(shape, 0))
        if pending is not None:
            apply(*pending, i * sub + u + 1)
        pending = fresh
    t_ref[...], rs_ref[...] = pending

    def raw(j):
        kb = k_ref[0, pl.ds(pl.multiple_of(j * tk, tk), tk), :].astype(BF16)
        return _dot_nt(qs, _stack_heads(kb))

    def body(n, c):
        j = (i - n) * sub - 1
        z = raw(j)
        apply(t_ref[...], rs_ref[...], j + 1)
        for u in range(1, sub):
            t, rowsum = _sb_reduce(z, w)
            z = raw(j - u)
            apply(t, rowsum, j - u + 1)
        t_ref[...], rs_ref[...] = _sb_reduce(z, w)
        return c

    lax.fori_loop(0, i, body, 0)
    apply(t_ref[...], rs_ref[...], 0)
    o_ref[0] = acc_ref[...].astype(o_ref.dtype)


def _prompt_attention_call(kernel, q, k, v, col0, n_col, extra_in, extra_specs, scratch, tq=ATT_TQ):
    b, t, _ = q.shape
    qspec = pl.BlockSpec((1, tq, LANES), lambda bi, c, i: (bi, i, col0 + c))
    kvspec = pl.BlockSpec((1, t, LANES), lambda bi, c, i: (bi, 0, col0 + c))
    return pl.pallas_call(
        kernel,
        grid=(b, n_col, t // tq),
        in_specs=list(extra_specs) + [qspec, kvspec, kvspec],
        out_specs=pl.BlockSpec((1, tq, LANES), lambda bi, c, i: (bi, i, c)),
        out_shape=jax.ShapeDtypeStruct((b, t, n_col * LANES), BF16),
        scratch_shapes=scratch,
        compiler_params=_params(("parallel", "parallel", "arbitrary")),
    )(*extra_in, q, k, v)


def _prompt_even_attention(q, k, v, lam_vecs, subln, lam_init):
    t = q.shape[1]
    tq = ATT_TQ
    whole = lambda shape: pl.BlockSpec(shape, lambda bi, c, i: (0, 0))
    vt = [pltpu.VMEM((t // ATT_KB, LANES, ATT_KB), BF16),
          pltpu.VMEM((2 * ATT_KB, tq), F32)]
    stats = [pltpu.VMEM((2, 1, tq), F32), pltpu.VMEM((2, 1, tq), F32)]
    sg = jnp.broadcast_to(subln.reshape(2 * HEAD_DIM, 1), (2 * HEAD_DIM, tq))
    o_a = _prompt_attention_call(
        functools.partial(_diff_prompt_kernel, lam_init=lam_init), q, k, v, 0, HALF // LANES,
        (lam_vecs, sg), (whole(lam_vecs.shape), whole(sg.shape)),
        vt + stats + [pltpu.VMEM((2, LANES, tq), F32)])
    n_blocks = t // MOBA_BLOCK
    o_b = _prompt_attention_call(
        _moba_prompt_kernel, q, k, v, HALF // LANES, HALF // LANES, (), (),
        vt + [pltpu.VMEM((2, n_blocks, LANES), F32), pltpu.VMEM((2, n_blocks, tq), F32)] + stats
        + [pltpu.VMEM((LANES, tq), F32)])
    return jnp.concatenate([o_a, o_b], axis=-1)


def _prompt_odd_attention(q, k, v):
    return _prompt_attention_call(
        _sb_prompt_kernel, q, k, v, 0, D_MODEL // LANES, (), (),
        [pltpu.VMEM((SB_TQ, 2 * SB_KB), F32)] * 2 + [pltpu.VMEM((SB_TQ, 2 * SB_TK), F32),
                                                     pltpu.VMEM((SB_TQ, LANES), F32)], tq=SB_TQ)


def _block_diag(q, n_heads, width):
    t = q.shape[0]
    tiled = jnp.concatenate([q] * n_heads, axis=0)
    keep = _head_mask(tiled.shape, t, width)
    return jnp.where(keep, tiled, 0.0)


def _head_mask(shape, rows_per_head, width):
    shift_r = rows_per_head.bit_length() - 1
    shift_c = width.bit_length() - 1
    return (_iota(shape, 0) >> shift_r) == (_iota(shape, 1) >> shift_c)


def _fold_heads(x, t):
    out = x[0:t]
    for h in range(1, x.shape[0] // t):
        out = out + x[h * t:(h + 1) * t]
    return out


def _pad_rows(x, rows):
    return jnp.concatenate([x, jnp.zeros((rows - x.shape[0], x.shape[1]), x.dtype)], axis=0)


def _even_decode_kernel(pt_ref, lam_ref, sg_ref, q_ref, kn_ref, vn_ref, *rest, lam_init, t_new):
    del pt_ref
    ck_refs, cv_refs = rest[:DEC_PAGES], rest[DEC_PAGES:2 * DEC_PAGES]
    (o_ref, qd_ref, qm_ref, qmf_ref, md_ref, ld_ref, accd_ref,
     mb_ref, lb_ref, accb_ref, gate_ref) = rest[2 * DEC_PAGES:]
    p = pl.program_id(1)
    n_steps = N_PAGES // DEC_PAGES
    n_maps = HALF // HEAD_DIM
    rows = n_maps * t_new
    blocks_per_step = DEC_PAGES * PAGE_SIZE // MOBA_BLOCK
    n_blocks = PAST_LEN // MOBA_BLOCK

    @pl.when(p == 0)
    def _():
        q = q_ref[0]
        qd_ref[...] = (_block_diag(q[:, :HALF], n_maps, HEAD_DIM) * SCALE).astype(BF16)
        qmf = _block_diag(q[:, HALF:], n_maps, HEAD_DIM)
        qmf_ref[...] = qmf
        qm_ref[...] = (qmf * SCALE).astype(BF16)
        md_ref[...] = jnp.full(md_ref.shape, NEG, F32)
        ld_ref[...] = jnp.zeros(ld_ref.shape, F32)
        accd_ref[...] = jnp.zeros(accd_ref.shape, F32)
        gate_ref[...] = jnp.zeros(gate_ref.shape, F32)

    def pages():
        kcat = jnp.concatenate([r[...] for r in ck_refs], axis=0)
        vcat = jnp.concatenate([r[...] for r in cv_refs], axis=0)
        s = _dot_nt(qd_ref[...], kcat[:, :HALF].astype(BF16))
        _softmax_step(s, vcat[:, :HALF].astype(BF16), md_ref, ld_ref, accd_ref)
        km = kcat[:, HALF:]
        vm = vcat[:, HALF:].astype(BF16)
        s = _dot_nt(qm_ref[...], km.astype(BF16))
        gate = gate_ref[...]
        for u in range(blocks_per_step):
            jb = p * blocks_per_step + u
            sb = s[:, u * MOBA_BLOCK:(u + 1) * MOBA_BLOCK]
            m = jnp.max(sb, axis=1, keepdims=True)
            pe = jnp.exp(sb - m)
            mb_ref[jb] = m
            lb_ref[jb] = jnp.sum(pe, axis=1, keepdims=True)
            accb_ref[jb] = _dot(pe.astype(BF16), vm[u * MOBA_BLOCK:(u + 1) * MOBA_BLOCK])
            kbar = jnp.sum(km[u * MOBA_BLOCK:(u + 1) * MOBA_BLOCK], axis=0, keepdims=True) * (1.0 / MOBA_BLOCK)
            g = jnp.sum(qmf_ref[...] * kbar, axis=1, keepdims=True)
            gate = jnp.where(_iota(gate.shape, 1) == jb, g, gate)
        gate_ref[...] = gate

    pages()

    @pl.when(p == n_steps - 1)
    def _():
        kpad = _pad_rows(kn_ref[0], PAGE_SIZE).astype(BF16)
        vpad = _pad_rows(vn_ref[0], PAGE_SIZE).astype(BF16)
        shape = (rows, PAGE_SIZE)
        causal = _iota(shape, 1) <= (_iota(shape, 0) & (t_new - 1))

        s = jnp.where(causal, _dot_nt(qd_ref[...], kpad[:, :HALF]), NEG)
        _softmax_step(s, vpad[:, :HALF], md_ref, ld_ref, accd_ref)
        od = accd_ref[...] / ld_ref[...]
        map_id = _iota(od.shape, 0) >> (t_new.bit_length() - 1)
        coef = jnp.where((map_id & 1) == 0, 1.0, -_lambda_value(lam_ref, lam_init))
        own = (map_id >> 1) == (_iota(od.shape, 1) >> ((2 * HEAD_DIM).bit_length() - 1))
        oa = _fold_heads(jnp.where(own, od * coef, 0.0), t_new)
        sg = sg_ref[...]
        for h in range(HALF // LANES):
            seg = oa[:, h * LANES:(h + 1) * LANES]
            o_ref[0, :, h * LANES:(h + 1) * LANES] = _rms(seg, sg) * (1.0 - lam_init)

        s = jnp.where(causal, _dot_nt(qm_ref[...], kpad[:, HALF:]), NEG)
        m_own = jnp.max(s, axis=1, keepdims=True)
        p_own = jnp.exp(s - m_own)
        l_own = jnp.sum(p_own, axis=1, keepdims=True)
        acc_own = _dot(p_own.astype(BF16), vpad[:, HALF:])
        gate = gate_ref[...]
        sel = _rank_select(gate, n_blocks, _iota(gate.shape, 1) < n_blocks)
        m_all = m_own
        for j in range(n_blocks):
            m_all = jnp.maximum(m_all, jnp.where(sel[:, j:j + 1], mb_ref[j], NEG))
        w_own = jnp.exp(m_own - m_all)
        num = w_own * acc_own
        den = w_own * l_own
        for j in range(n_blocks):
            w = jnp.where(sel[:, j:j + 1], jnp.exp(mb_ref[j] - m_all), 0.0)
            num = num + w * accb_ref[j]
            den = den + w * lb_ref[j]
        ob = jnp.where(_head_mask(num.shape, t_new, HEAD_DIM), num / den, 0.0)
        o_ref[0, :, HALF:] = _fold_heads(ob, t_new)


def _odd_decode_kernel(pt_ref, q_ref, kn_ref, vn_ref, *rest, t_new):
    del pt_ref
    ck_refs, cv_refs = rest[:DEC_PAGES], rest[DEC_PAGES:2 * DEC_PAGES]
    o_ref, qs_ref, carry_ref, acc_ref = rest[2 * DEC_PAGES:]
    p = pl.program_id(1)
    n_heads = D_MODEL // HEAD_DIM
    rows = n_heads * t_new

    def page(k_bf, v_bf, strict, w):
        z = _dot_nt(qs_ref[...], k_bf)
        sp = _softplus(z)
        if strict is not None:
            sp = jnp.where(strict, sp, 0.0)
        suffix, rowsum = _sb_suffix(sp, w)
        return z, suffix, rowsum

    @pl.when(p == 0)
    def _():
        qs_ref[...] = (_block_diag(q_ref[0], n_heads, HEAD_DIM) * SCALE).astype(BF16)
        acc_ref[...] = jnp.zeros(acc_ref.shape, F32)
        kpad = _pad_rows(kn_ref[0], PAGE_SIZE).astype(BF16)
        vpad = _pad_rows(vn_ref[0], PAGE_SIZE).astype(BF16)
        shape = (rows, PAGE_SIZE)
        strict = _iota(shape, 1) < (_iota(shape, 0) & (t_new - 1))
        z, suffix, rowsum = page(kpad, vpad, strict, _sb_weights())
        a = jnp.where(strict, jnp.exp(z - suffix), 0.0)
        acc_ref[...] += _dot(a.astype(BF16), vpad)
        carry_ref[...] = rowsum

    def pages():
        w = _sb_weights()
        carry = carry_ref[...]
        weights = []
        for u in range(DEC_PAGES):
            z, suffix, rowsum = page(ck_refs[u][...].astype(BF16), None, None, w)
            weights.append(jnp.exp(z - suffix - carry).astype(BF16))
            carry = carry + rowsum
        carry_ref[...] = carry
        vcat = jnp.concatenate([r[...] for r in cv_refs], axis=0).astype(BF16)
        acc_ref[...] += _dot(jnp.concatenate(weights, axis=1), vcat)

    pages()

    @pl.when(p == N_PAGES // DEC_PAGES - 1)
    def _():
        acc = acc_ref[...]
        o_ref[0] = _fold_heads(jnp.where(_head_mask(acc.shape, t_new, HEAD_DIM), acc, 0.0), t_new)


def _decode_attention(kernel, q, k_new, v_new, cache_k, cache_v, page_table, layer, page_of, extra_in, scratch):
    b, t_new, _ = q.shape
    new = pl.BlockSpec((1, t_new, D_MODEL), lambda bi, p, pt: (bi, 0, 0))

    def page(u):
        return pl.BlockSpec((None, None, PAGE_SIZE, D_MODEL),
                            lambda bi, p, pt: (layer, pt[bi, page_of(p, u)], 0, 0))

    pages = [page(u) for u in range(DEC_PAGES)]
    extra_specs = [pl.BlockSpec(a.shape, lambda bi, p, pt: (0, 0)) for a in extra_in]
    return pl.pallas_call(
        kernel,
        grid_spec=pltpu.PrefetchScalarGridSpec(
            num_scalar_prefetch=1,
            grid=(b, N_PAGES // DEC_PAGES),
            in_specs=extra_specs + [new, new, new] + pages + pages,
            out_specs=new,
            scratch_shapes=scratch,
        ),
        out_shape=jax.ShapeDtypeStruct((b, t_new, D_MODEL), F32),
        compiler_params=_params(("parallel", "arbitrary")),
    )(page_table, *extra_in, q, k_new, v_new, *([cache_k] * DEC_PAGES), *([cache_v] * DEC_PAGES))


def _decode_even_attention(q, k_new, v_new, cache_k, cache_v, page_table, layer, lam_vecs, subln, lam_init):
    t_new = q.shape[1]
    rows = (HALF // HEAD_DIM) * t_new
    n_blocks = PAST_LEN // MOBA_BLOCK
    n_steps = N_PAGES // DEC_PAGES
    scratch = [
        pltpu.VMEM((rows, HALF), BF16), pltpu.VMEM((rows, HALF), BF16), pltpu.VMEM((rows, HALF), F32),
        pltpu.VMEM((rows, 1), F32), pltpu.VMEM((rows, 1), F32), pltpu.VMEM((rows, HALF), F32),
        pltpu.VMEM((n_blocks, rows, 1), F32), pltpu.VMEM((n_blocks, rows, 1), F32),
        pltpu.VMEM((n_blocks, rows, HALF), F32),
        pltpu.VMEM((rows, LANES), F32),
    ]
    page_of = lambda p, u: p * DEC_PAGES + u
    return _decode_attention(
        functools.partial(_even_decode_kernel, lam_init=lam_init, t_new=t_new),
        q, k_new, v_new, cache_k, cache_v, page_table, layer, page_of, (lam_vecs, subln), scratch)


def _decode_odd_attention(q, k_new, v_new, cache_k, cache_v, page_table, layer):
    t_new = q.shape[1]
    rows = (D_MODEL // HEAD_DIM) * t_new
    scratch = [pltpu.VMEM((rows, D_MODEL), BF16), pltpu.VMEM((rows, LANES), F32), pltpu.VMEM((rows, D_MODEL), F32)]
    page_of = lambda p, u: N_PAGES - p * DEC_PAGES - 1 - u
    return _decode_attention(
        functools.partial(_odd_decode_kernel, t_new=t_new),
        q, k_new, v_new, cache_k, cache_v, page_table, layer, page_of, (), scratch)


def _rope_tables(pos):
    half = HEAD_DIM // 2
    inv_freq = 1.0 / (ROPE_THETA ** (jnp.arange(half, dtype=F32) * (2.0 / HEAD_DIM)))
    ang = pos.astype(F32)[:, None] * inv_freq[None, :]
    cos, sin, zero = jnp.cos(ang), jnp.sin(ang), jnp.zeros_like(ang)
    reps = LANES // HEAD_DIM
    return (jnp.concatenate([cos, cos] * reps, axis=1),
            jnp.concatenate([-sin, zero] * reps, axis=1),
            jnp.concatenate([zero, sin] * reps, axis=1))


def _lambda_init(layer):
    return 0.8 - 0.6 * math.exp(-0.3 * layer)


def _trunk(x3, tables, cache, weights):
    b, t, _ = x3.shape
    n = b * t
    x = x3.reshape(n, D_MODEL)
    gains = weights["norm_gains"]
    new_k, new_v = [], []
    for l in range(DEPTH):
        g = [gains[l, i].reshape(1, D_MODEL) for i in range(6)]
        x = _ffn(x, g[0], g[1], weights["ffn1_w_in"], weights["ffn1_w_out"], l)
        even = l % 2 == 0
        q, k, v = _inproj(x, g[2], weights["mix_w_in"], l, tables, even)
        q3, k3, v3 = (a.reshape(b, t, D_MODEL) for a in (q, k, v))
        if even:
            lam_vecs = weights["diff_lambda"][l // 2]
            subln = weights["diff_subln"][l // 2].reshape(1, 2 * HEAD_DIM)
            if cache is None:
                o = _prompt_even_attention(q3, k3, v3, lam_vecs, subln, _lambda_init(l))
            else:
                o = _decode_even_attention(q3, k3, v3, *cache, l, lam_vecs, subln, _lambda_init(l))
        else:
            if cache is None:
                o = _prompt_odd_attention(q3, k3, v3)
            else:
                o = _decode_odd_attention(q3, k3, v3, *cache, l)
        x = _outproj(o.reshape(n, D_MODEL).astype(BF16), x, weights["mix_w_out"], g[3], l)
        x = _ffn(x, g[4], g[5], weights["ffn2_w_in"], weights["ffn2_w_out"], l)
        new_k.append(k3)
        new_v.append(v3)
    return x.reshape(b, t, D_MODEL), jnp.stack(new_k), jnp.stack(new_v)


def _even_layer_in_proj(w):
    return jnp.concatenate([w[:, c * HALF:(c + 1) * HALF] for c in (0, 3, 1, 4, 2, 5)], axis=1)


def kernel(x_prompt, x_sample, cache_k, cache_v, page_table, norm_gains, ffn1_w_in, ffn1_w_out, mix_w_in,
           mix_w_out, diff_lambda, diff_subln, ffn2_w_in, ffn2_w_out):
    mix_in = jnp.stack([_even_layer_in_proj(mix_w_in[l]) if l % 2 == 0 else mix_w_in[l] for l in range(DEPTH)])
    weights = {
        "norm_gains": norm_gains,
        "ffn1_w_in": ffn1_w_in.astype(BF16), "ffn1_w_out": ffn1_w_out.astype(BF16),
        "ffn2_w_in": ffn2_w_in.astype(BF16), "ffn2_w_out": ffn2_w_out.astype(BF16),
        "mix_w_in": mix_in.astype(BF16), "mix_w_out": mix_w_out.astype(BF16),
        "diff_lambda": diff_lambda, "diff_subln": diff_subln,
    }
    seq = x_prompt.shape[1]
    t_new = x_sample.shape[1]
    prompt_tables = _rope_tables(jnp.arange(seq, dtype=jnp.int32))
    sample_pos = PAST_LEN + (jnp.arange(FFN_TM, dtype=jnp.int32) % t_new)
    sample_tables = _rope_tables(sample_pos)

    y_prompt, k_prompt, v_prompt = _trunk(x_prompt, prompt_tables, None, weights)
    y_sample, k_sample, v_sample = _trunk(x_sample, sample_tables, (cache_k, cache_v, page_table), weights)
    return (y_prompt, y_sample, k_prompt, v_prompt, k_sample, v_sample)
```

```python
import functools
import math

import jax
import jax.numpy as jnp
from jax import lax
from jax.experimental import pallas as pl
from jax.experimental.pallas import tpu as pltpu

F32 = jnp.float32
BF16 = jnp.bfloat16

D_MODEL = 1024
HEAD_DIM = 64
D_FF = 2816
DEPTH = 4
PAGE_SIZE = 128
PAST_LEN = 2048
N_PAGES = PAST_LEN // PAGE_SIZE
MOBA_BLOCK = 256
MOBA_TOPK = 3
ROPE_THETA = 10000.0
NORM_EPS = 1e-6
SCALE = HEAD_DIM ** -0.5
HALF = D_MODEL // 2
LANES = 128
NEG = -1e30

VMEM_LIMIT = 56 * 1024 * 1024

FFN_TM = 512
FFN_TF = 1408
ATT_KB = MOBA_BLOCK
ATT_TQ = 2 * ATT_KB
SB_TK = 128
SB_KB = 256
SB_TQ = 512
DEC_PAGES = 8


def _params(sem):
    return pltpu.CompilerParams(dimension_semantics=sem, vmem_limit_bytes=VMEM_LIMIT)


def _rms(x, g):
    return x * lax.rsqrt(jnp.mean(x * x, axis=-1, keepdims=True) + NORM_EPS) * g


def _dot_nt(a, b):
    return lax.dot_general(a, b, (((1,), (1,)), ((), ())), preferred_element_type=F32)


def _dot(a, b):
    return jnp.dot(a, b, preferred_element_type=F32)


def _iota(shape, dim):
    return lax.broadcasted_iota(jnp.int32, shape, dim)


def _ffn_kernel(x_ref, gpre_ref, gpost_ref, wg_ref, wu_ref, wo_ref, o_ref, xn_ref, acc_ref):
    j = pl.program_id(1)

    @pl.when(j == 0)
    def _():
        xn_ref[...] = _rms(x_ref[...], gpre_ref[...]).astype(BF16)

    xn = xn_ref[...]
    gate = _dot(xn, wg_ref[...])
    up = _dot(xn, wu_ref[...])
    h = (gate * jax.nn.sigmoid(gate) * up).astype(BF16)
    part = _dot(h, wo_ref[...])

    @pl.when(j == 0)
    def _():
        acc_ref[...] = part

    @pl.when(j > 0)
    def _():
        acc_ref[...] += part

    @pl.when(j == pl.num_programs(1) - 1)
    def _():
        o_ref[...] = x_ref[...] + 0.5 * _rms(acc_ref[...], gpost_ref[...])


def _ffn(x, g_pre, g_post, w_in, w_out, layer):
    n = x.shape[0]
    nf = D_FF // FFN_TF
    return pl.pallas_call(
        _ffn_kernel,
        grid=(n // FFN_TM, nf),
        in_specs=[
            pl.BlockSpec((FFN_TM, D_MODEL), lambda i, j: (i, 0)),
            pl.BlockSpec((1, D_MODEL), lambda i, j: (0, 0)),
            pl.BlockSpec((1, D_MODEL), lambda i, j: (0, 0)),
            pl.BlockSpec((None, D_MODEL, FFN_TF), lambda i, j: (layer, 0, j)),
            pl.BlockSpec((None, D_MODEL, FFN_TF), lambda i, j: (layer, 0, j + nf)),
            pl.BlockSpec((None, FFN_TF, D_MODEL), lambda i, j: (layer, j, 0)),
        ],
        out_specs=pl.BlockSpec((FFN_TM, D_MODEL), lambda i, j: (i, 0)),
        out_shape=jax.ShapeDtypeStruct((n, D_MODEL), F32),
        scratch_shapes=[pltpu.VMEM((FFN_TM, D_MODEL), BF16), pltpu.VMEM((FFN_TM, D_MODEL), F32)],
        compiler_params=_params(("parallel", "arbitrary")),
    )(x, g_pre, g_post, w_in, w_in, w_out)


def _inproj_kernel(x_ref, g_ref, w_ref, cos_ref, sina_ref, sinb_ref, q_ref, k_ref, v_ref, *, rotary):
    xn = _rms(x_ref[...], g_ref[...]).astype(BF16)
    outs = (q_ref, k_ref, v_ref)
    for c in range(3):
        y = _dot(xn, w_ref[:, c * D_MODEL:(c + 1) * D_MODEL])
        if rotary and c < 2:
            cos, sina, sinb = cos_ref[...], sina_ref[...], sinb_ref[...]
            for p in range(D_MODEL // LANES):
                yp = y[:, p * LANES:(p + 1) * LANES]
                fwd = pltpu.roll(yp, LANES - HEAD_DIM // 2, 1)
                bwd = pltpu.roll(yp, HEAD_DIM // 2, 1)
                outs[c][:, p * LANES:(p + 1) * LANES] = yp * cos + fwd * sina + bwd * sinb
        else:
            outs[c][...] = y


def _inproj(x, g, w, layer, tables, rotary):
    n = x.shape[0]
    cos, sina, sinb = tables
    nt = cos.shape[0] // FFN_TM
    tab = pl.BlockSpec((FFN_TM, LANES), lambda i: (i % nt, 0))
    row = pl.BlockSpec((FFN_TM, D_MODEL), lambda i: (i, 0))
    return pl.pallas_call(
        functools.partial(_inproj_kernel, rotary=rotary),
        grid=(n // FFN_TM,),
        in_specs=[
            row,
            pl.BlockSpec((1, D_MODEL), lambda i: (0, 0)),
            pl.BlockSpec((None, D_MODEL, 3 * D_MODEL), lambda i: (layer, 0, 0)),
            tab, tab, tab,
        ],
        out_specs=[row, row, row],
        out_shape=[jax.ShapeDtypeStruct((n, D_MODEL), F32)] * 3,
        compiler_params=_params(("parallel",)),
    )(x, g, w, cos, sina, sinb)


def _outproj_kernel(o_ref, x_ref, w_ref, g_ref, y_ref):
    m = _dot(o_ref[...], w_ref[...])
    y_ref[...] = x_ref[...] + _rms(m, g_ref[...])


def _outproj(o, x, w, g, layer):
    n = x.shape[0]
    row = pl.BlockSpec((FFN_TM, D_MODEL), lambda i: (i, 0))
    return pl.pallas_call(
        _outproj_kernel,
        grid=(n // FFN_TM,),
        in_specs=[
            row, row,
            pl.BlockSpec((None, D_MODEL, D_MODEL), lambda i: (layer, 0, 0)),
            pl.BlockSpec((1, D_MODEL), lambda i: (0, 0)),
        ],
        out_specs=row,
        out_shape=jax.ShapeDtypeStruct((n, D_MODEL), F32),
        compiler_params=_params(("parallel",)),
    )(o, x, w, g)


def _stack_heads(x):
    low = _iota(x.shape, 1) < HEAD_DIM
    return jnp.concatenate([jnp.where(low, x, 0.0), jnp.where(low, 0.0, x)], axis=0)


def _softmax_step(s, vb, m_ref, l_ref, acc_ref):
    m_old = m_ref[...]
    m_new = jnp.maximum(m_old, jnp.max(s, axis=1, keepdims=True))
    alpha = jnp.exp(m_old - m_new)
    p = jnp.exp(s - m_new)
    l_ref[...] = alpha * l_ref[...] + jnp.sum(p, axis=1, keepdims=True)
    acc_ref[...] = alpha * acc_ref[...] + _dot(p.astype(BF16), vb)
    m_ref[...] = m_new


def _softmax_step_t(s, m_ref, l_ref):
    m_old = m_ref[...]
    m_new = jnp.maximum(m_old, jnp.max(s, axis=0, keepdims=True))
    alpha = jnp.exp(m_old - m_new)
    p = jnp.exp(s - m_new)
    l_ref[...] = alpha * l_ref[...] + jnp.sum(p, axis=0, keepdims=True)
    m_ref[...] = m_new
    return alpha, p.astype(BF16)


def _lambda_value(lam_ref, lam_init):
    lv = lam_ref[...]
    a = jnp.sum(lv[0:1] * lv[1:2], axis=1, keepdims=True)
    b = jnp.sum(lv[2:3] * lv[3:4], axis=1, keepdims=True)
    return jnp.exp(a) - jnp.exp(b) + lam_init


def _sb_weights():
    shape = (2 * SB_TK, 2 * SB_TK)
    j = _iota(shape, 0) & (SB_TK - 1)
    s = _iota(shape, 1)
    return jnp.where((s >= SB_TK) | (j >= s), 1.0, 0.0).astype(BF16)


def _sb_suffix(sp, w):
    hi = sp.astype(BF16)
    lo = (sp - hi.astype(F32)).astype(BF16)
    r = _dot(jnp.concatenate([hi, lo], axis=1), w)
    return r[:, :SB_TK], r[:, SB_TK:]


def _softplus(z):
    return jnp.maximum(z, 0.0) + jnp.log(1.0 + jnp.exp(-jnp.abs(z)))


def _rank_select_t(gate, eligible):
    blk = _iota(gate.shape, 0)
    g = jnp.where(eligible, gate, -jnp.inf)
    cnt = jnp.zeros(gate.shape, jnp.int32)
    for jp in range(gate.shape[0]):
        row = g[jp:jp + 1, :]
        beats = (row > g) | ((row == g) & (jp < blk))
        cnt = cnt + jnp.where(beats, 1, 0)
    return (cnt < MOBA_TOPK) & eligible


def _rank_select(gate, n_blocks, eligible):
    lane = _iota(gate.shape, 1)
    g = jnp.where(eligible, gate, -jnp.inf)
    cnt = jnp.zeros(gate.shape, jnp.int32)
    for jp in range(n_blocks):
        col = g[:, jp:jp + 1]
        beats = (col > g) | ((col == g) & (jp < lane))
        cnt = cnt + jnp.where(beats, 1, 0)
    return (cnt < MOBA_TOPK) & eligible


def _store_v_transposed(v_ref, vt_ref):
    tk = vt_ref.shape[2]
    for c in range(vt_ref.shape[0]):
        vt_ref[c] = v_ref[0, c * tk:(c + 1) * tk, :].T.astype(BF16)


def _own_causal_t(shape, u):
    return (u * ATT_KB + (_iota(shape, 0) & (ATT_KB - 1))) <= _iota(shape, 1)


def _pipelined_blocks(i, scores, consume, st_ref):
    st_ref[...] = scores(2 * i + 1, 1)
    own = scores(2 * i, 0)
    consume(st_ref[...], 2 * i + 1)
    st_ref[...] = own

    def pair(n, c):
        j = 2 * n
        first = scores(j, None)
        consume(st_ref[...], jnp.where(j == 0, 2 * i, j - 1))
        st_ref[...] = scores(j + 1, None)
        consume(first, j)
        return c

    lax.fori_loop(0, i, pair, 0)
    consume(st_ref[...], jnp.maximum(2 * i - 1, 0))


def _diff_prompt_kernel(lam_ref, sg_ref, q_ref, k_ref, v_ref, o_ref, vt_ref, st_ref, m_ref, l_ref, acc_ref, *,
                        lam_init):
    i = pl.program_id(2)
    tk = ATT_KB

    @pl.when(i == 0)
    def _():
        _store_v_transposed(v_ref, vt_ref)

    qs = (q_ref[0] * SCALE).astype(BF16)
    m_ref[...] = jnp.full(m_ref.shape, NEG, F32)
    l_ref[...] = jnp.zeros(l_ref.shape, F32)
    acc_ref[...] = jnp.zeros(acc_ref.shape, F32)

    def scores(j, own):
        kb = k_ref[0, pl.ds(pl.multiple_of(j * tk, tk), tk), :].astype(BF16)
        st = _dot_nt(_stack_heads(kb), qs)
        return st if own is None else jnp.where(_own_causal_t(st.shape, own), st, NEG)

    def consume(st, j):
        vt = vt_ref[j]
        for mp in range(2):
            alpha, p = _softmax_step_t(st[mp * tk:(mp + 1) * tk], m_ref.at[mp], l_ref.at[mp])
            acc_ref[mp] = alpha * acc_ref[mp] + _dot(vt, p)

    _pipelined_blocks(i, scores, consume, st_ref)

    a = acc_ref[0] / l_ref[0] - _lambda_value(lam_ref, lam_init) * (acc_ref[1] / l_ref[1])
    y = a * lax.rsqrt(jnp.mean(a * a, axis=0, keepdims=True) + NORM_EPS) * sg_ref[...] * (1.0 - lam_init)
    o_ref[0] = y.T.astype(o_ref.dtype)


def _moba_prompt_kernel(q_ref, k_ref, v_ref, o_ref, vt_ref, st_ref, kbar_ref, bias_ref, m_ref, l_ref, acc_ref):
    i = pl.program_id(2)
    tq, tk = ATT_TQ, ATT_KB
    n_blocks = k_ref.shape[1] // tk

    @pl.when(i == 0)
    def _():
        _store_v_transposed(v_ref, vt_ref)
        low = _iota((1, LANES), 1) < HEAD_DIM
        for jb in range(n_blocks):
            blk = k_ref[0, jb * tk:(jb + 1) * tk, :]
            kbar = jnp.sum(blk, axis=0, keepdims=True) * (1.0 / tk)
            kbar_ref[0, jb:jb + 1, :] = jnp.where(low, kbar, 0.0)
            kbar_ref[1, jb:jb + 1, :] = jnp.where(low, 0.0, kbar)

    q = q_ref[0]
    own_block = _iota((n_blocks, tq), 1) >> (tk.bit_length() - 1)
    for h in range(2):
        gate = lax.dot_general(kbar_ref[h], q, (((1,), (1,)), ((), ())),
                               precision=lax.Precision.HIGHEST, preferred_element_type=F32)
        sel = _rank_select_t(gate, _iota(gate.shape, 0) < 2 * i + own_block)
        bias_ref[h] = jnp.where(sel, 0.0, NEG)
    qs = (q * SCALE).astype(BF16)

    m_ref[...] = jnp.full(m_ref.shape, NEG, F32)
    l_ref[...] = jnp.zeros(l_ref.shape, F32)
    acc_ref[...] = jnp.zeros(acc_ref.shape, F32)
    head_a = _iota((LANES, tq), 0) < HEAD_DIM

    def scores(j, own):
        kb = k_ref[0, pl.ds(pl.multiple_of(j * tk, tk), tk), :].astype(BF16)
        st = _dot_nt(_stack_heads(kb), qs)
        biased = jnp.concatenate([st[h * tk:(h + 1) * tk] + bias_ref[h, pl.ds(j, 1), :] for h in range(2)], axis=0)
        if own is None:
            return biased
        inside = (_iota(st.shape, 1) >> (tk.bit_length() - 1)) == own
        return jnp.where(inside, jnp.where(_own_causal_t(st.shape, own), st, NEG), biased)

    def consume(st, j):
        alphas, ps = [], []
        for h in range(2):
            alpha, p = _softmax_step_t(st[h * tk:(h + 1) * tk], m_ref.at[h], l_ref.at[h])
            alphas.append(alpha)
            ps.append(p)
        vt = vt_ref[j]
        keep = _iota(vt.shape, 0) < HEAD_DIM
        vbd = jnp.concatenate([jnp.where(keep, vt, 0.0), jnp.where(keep, 0.0, vt)], axis=1)
        alpha = jnp.where(head_a, alphas[0], alphas[1])
        acc_ref[...] = alpha * acc_ref[...] + _dot(vbd, jnp.concatenate(ps, axis=0))

    _pipelined_blocks(i, scores, consume, st_ref)

    o = acc_ref[...] / jnp.where(head_a, l_ref[0], l_ref[1])
    o_ref[0] = o.T.astype(o_ref.dtype)


def _sb_scores(qs, kb, w, strict):
    z = _dot_nt(qs, _stack_heads(kb))
    if strict is not None:
        z = jnp.where(strict, z, NEG)
    return _sb_reduce(z, w)


def _sb_reduce(z, w):
    sp = _softplus(z)
    parts = [_sb_suffix(sp[:, g * SB_TK:(g + 1) * SB_TK], w) for g in range(z.shape[1] // SB_TK)]
    suffix = jnp.concatenate([p[0] for p in parts], axis=1)
    rowsum = jnp.concatenate([p[1] for p in parts], axis=1)
    return z - suffix, rowsum


def _sb_apply(t, rowsum, vb, carry_ref, acc_ref):
    groups = t.shape[1] // (2 * SB_TK)
    carry = carry_ref[...]
    weights = [None] * (2 * groups)
    carries = []
    for h in range(2):
        c = carry[:, h * SB_TK:(h + 1) * SB_TK]
        for u in reversed(range(groups)):
            g = h * groups + u
            weights[g] = jnp.exp(t[:, g * SB_TK:(g + 1) * SB_TK] - c).astype(BF16)
            c = c + rowsum[:, g * SB_TK:(g + 1) * SB_TK]
        carries.append(c)
    acc_ref[...] += _dot(jnp.concatenate(weights, axis=1), _stack_heads(vb))
    carry_ref[...] = jnp.concatenate(carries, axis=1)


def _sb_prompt_kernel(q_ref, k_ref, v_ref, o_ref, t_ref, rs_ref, carry_ref, acc_ref):
    i = pl.program_id(2)
    tq, tk = SB_TQ, SB_KB
    sub = tq // tk
    qs = (q_ref[0] * SCALE).astype(BF16)
    w = _sb_weights()
    carry_ref[...] = jnp.zeros(carry_ref.shape, F32)
    acc_ref[...] = jnp.zeros(acc_ref.shape, F32)

    def apply(t, rowsum, j):
        vb = v_ref[0, pl.ds(pl.multiple_of(j * tk, tk), tk), :].astype(BF16)
        _sb_apply(t, rowsum, vb, carry_ref, acc_ref)

    pending = None
    for u in reversed(range(sub)):
        r0 = u * tk
        shape = (tq - r0, 2 * tk)
        strict = (_iota(shape, 1) & (tk - 1)) < _iota(shape, 0)
        kb = k_ref[0, pl.ds(pl.multiple_of((i * sub + u) * tk, tk), tk), :].astype(BF16)
        fresh = _sb_scores(qs[r0:], kb, w, strict)
        if pending is not None:
            rows = pl.ds(r0 + tk, tq - r0 - tk)
            vb = v_ref[0, pl.ds(pl.multiple_of((i * sub + u + 1) * tk, tk), tk), :].astype(BF16)
            _sb_apply(*pending, vb, carry_ref.at[rows], acc_ref.at[rows])
        pending = fresh
    t_ref[...], rs_ref[...] = pending

    def raw(j):
        kb = k_ref[0, pl.ds(pl.multiple_of(j * tk, tk), tk), :].astype(BF16)
        return _dot_nt(qs, _stack_heads(kb))

    def body(n, c):
        j = (i - n) * sub - 1
        z = raw(j)
        apply(t_ref[...], rs_ref[...], j + 1)
        for u in range(1, sub):
            t, rowsum = _sb_reduce(z, w)
            z = raw(j - u)
            apply(t, rowsum, j - u + 1)
        t_ref[...], rs_ref[...] = _sb_reduce(z, w)
        return c

    lax.fori_loop(0, i, body, 0)
    apply(t_ref[...], rs_ref[...], 0)
    o_ref[0] = acc_ref[...].astype(o_ref.dtype)


def _prompt_attention_call(kernel, q, k, v, col0, n_col, extra_in, extra_specs, scratch, tq=ATT_TQ):
    b, t, _ = q.shape
    qspec = pl.BlockSpec((1, tq, LANES), lambda bi, c, i: (bi, i, col0 + c))
    kvspec = pl.BlockSpec((1, t, LANES), lambda bi, c, i: (bi, 0, col0 + c))
    return pl.pallas_call(
        kernel,
        grid=(b, n_col, t // tq),
        in_specs=list(extra_specs) + [qspec, kvspec, kvspec],
        out_specs=pl.BlockSpec((1, tq, LANES), lambda bi, c, i: (bi, i, c)),
        out_shape=jax.ShapeDtypeStruct((b, t, n_col * LANES), BF16),
        scratch_shapes=scratch,
        compiler_params=_params(("parallel", "parallel", "arbitrary")),
    )(*extra_in, q, k, v)


def _prompt_even_attention(q, k, v, lam_vecs, subln, lam_init):
    t = q.shape[1]
    tq = ATT_TQ
    whole = lambda shape: pl.BlockSpec(shape, lambda bi, c, i: (0, 0))
    vt = [pltpu.VMEM((t // ATT_KB, LANES, ATT_KB), BF16),
          pltpu.VMEM((2 * ATT_KB, tq), F32)]
    stats = [pltpu.VMEM((2, 1, tq), F32), pltpu.VMEM((2, 1, tq), F32)]
    sg = jnp.broadcast_to(subln.reshape(2 * HEAD_DIM, 1), (2 * HEAD_DIM, tq))
    o_a = _prompt_attention_call(
        functools.partial(_diff_prompt_kernel, lam_init=lam_init), q, k, v, 0, HALF // LANES,
        (lam_vecs, sg), (whole(lam_vecs.shape), whole(sg.shape)),
        vt + stats + [pltpu.VMEM((2, LANES, tq), F32)])
    n_blocks = t // MOBA_BLOCK
    o_b = _prompt_attention_call(
        _moba_prompt_kernel, q, k, v, HALF // LANES, HALF // LANES, (), (),
        vt + [pltpu.VMEM((2, n_blocks, LANES), F32), pltpu.VMEM((2, n_blocks, tq), F32)] + stats
        + [pltpu.VMEM((LANES, tq), F32)])
    return jnp.concatenate([o_a, o_b], axis=-1)


def _prompt_odd_attention(q, k, v):
    return _prompt_attention_call(
        _sb_prompt_kernel, q, k, v, 0, D_MODEL // LANES, (), (),
        [pltpu.VMEM((SB_TQ, 2 * SB_KB), F32)] * 2 + [pltpu.VMEM((SB_TQ, 2 * SB_TK), F32),
                                                     pltpu.VMEM((SB_TQ, LANES), F32)], tq=SB_TQ)


def _block_diag(q, n_heads, width):
    t = q.shape[0]
    tiled = jnp.concatenate([q] * n_heads, axis=0)
    keep = _head_mask(tiled.shape, t, width)
    return jnp.where(keep, tiled, 0.0)


def _head_mask(shape, rows_per_head, width):
    shift_r = rows_per_head.bit_length() - 1
    shift_c = width.bit_length() - 1
    return (_iota(shape, 0) >> shift_r) == (_iota(shape, 1) >> shift_c)


def _fold_heads(x, t):
    out = x[0:t]
    for h in range(1, x.shape[0] // t):
        out = out + x[h * t:(h + 1) * t]
    return out


def _pad_rows(x, rows):
    return jnp.concatenate([x, jnp.zeros((rows - x.shape[0], x.shape[1]), x.dtype)], axis=0)


def _even_decode_kernel(pt_ref, lam_ref, sg_ref, q_ref, kn_ref, vn_ref, *rest, lam_init, t_new):
    del pt_ref
    ck_refs, cv_refs = rest[:DEC_PAGES], rest[DEC_PAGES:2 * DEC_PAGES]
    (o_ref, qd_ref, qm_ref, qmf_ref, md_ref, ld_ref, accd_ref,
     accb_ref, gate_ref, mrow_ref, lrow_ref) = rest[2 * DEC_PAGES:]
    p = pl.program_id(1)
    n_steps = N_PAGES // DEC_PAGES
    n_maps = HALF // HEAD_DIM
    rows = n_maps * t_new
    blocks_per_step = DEC_PAGES * PAGE_SIZE // MOBA_BLOCK
    n_blocks = PAST_LEN // MOBA_BLOCK

    @pl.when(p == 0)
    def _():
        q = q_ref[0]
        qd_ref[...] = (_block_diag(q[:, :HALF], n_maps, HEAD_DIM) * SCALE).astype(BF16)
        qmf = _block_diag(q[:, HALF:], n_maps, HEAD_DIM)
        qmf_ref[...] = qmf
        qm_ref[...] = (qmf * SCALE).astype(BF16)
        md_ref[...] = jnp.full(md_ref.shape, NEG, F32)
        ld_ref[...] = jnp.zeros(ld_ref.shape, F32)
        accd_ref[...] = jnp.zeros(accd_ref.shape, F32)
        gate_ref[...] = jnp.zeros(gate_ref.shape, F32)
        mrow_ref[...] = jnp.zeros(mrow_ref.shape, F32)
        lrow_ref[...] = jnp.zeros(lrow_ref.shape, F32)

    def pages():
        kcat = jnp.concatenate([r[...] for r in ck_refs], axis=0)
        vcat = jnp.concatenate([r[...] for r in cv_refs], axis=0)
        s = _dot_nt(qd_ref[...], kcat[:, :HALF].astype(BF16))
        _softmax_step(s, vcat[:, :HALF].astype(BF16), md_ref, ld_ref, accd_ref)
        km = kcat[:, HALF:]
        vm = vcat[:, HALF:].astype(BF16)
        s = _dot_nt(qm_ref[...], km.astype(BF16))
        gate, mrow, lrow = gate_ref[...], mrow_ref[...], lrow_ref[...]
        for u in range(blocks_per_step):
            jb = p * blocks_per_step + u
            sb = s[:, u * MOBA_BLOCK:(u + 1) * MOBA_BLOCK]
            m = jnp.max(sb, axis=1, keepdims=True)
            pe = jnp.exp(sb - m)
            accb_ref[jb] = _dot(pe.astype(BF16), vm[u * MOBA_BLOCK:(u + 1) * MOBA_BLOCK])
            kbar = jnp.sum(km[u * MOBA_BLOCK:(u + 1) * MOBA_BLOCK], axis=0, keepdims=True) * (1.0 / MOBA_BLOCK)
            g = jnp.sum(qmf_ref[...] * kbar, axis=1, keepdims=True)
            here = _iota(gate.shape, 1) == jb
            gate = jnp.where(here, g, gate)
            mrow = jnp.where(here, m, mrow)
            lrow = jnp.where(here, jnp.sum(pe, axis=1, keepdims=True), lrow)
        gate_ref[...], mrow_ref[...], lrow_ref[...] = gate, mrow, lrow

    pages()

    @pl.when(p == n_steps - 1)
    def _():
        kpad = _pad_rows(kn_ref[0], PAGE_SIZE).astype(BF16)
        vpad = _pad_rows(vn_ref[0], PAGE_SIZE).astype(BF16)
        shape = (rows, PAGE_SIZE)
        causal = _iota(shape, 1) <= (_iota(shape, 0) & (t_new - 1))

        s = jnp.where(causal, _dot_nt(qd_ref[...], kpad[:, :HALF]), NEG)
        _softmax_step(s, vpad[:, :HALF], md_ref, ld_ref, accd_ref)
        od = accd_ref[...] / ld_ref[...]
        map_id = _iota(od.shape, 0) >> (t_new.bit_length() - 1)
        coef = jnp.where((map_id & 1) == 0, 1.0, -_lambda_value(lam_ref, lam_init))
        own = (map_id >> 1) == (_iota(od.shape, 1) >> ((2 * HEAD_DIM).bit_length() - 1))
        oa = _fold_heads(jnp.where(own, od * coef, 0.0), t_new)
        sg = sg_ref[...]
        for h in range(HALF // LANES):
            seg = oa[:, h * LANES:(h + 1) * LANES]
            o_ref[0, :, h * LANES:(h + 1) * LANES] = _rms(seg, sg) * (1.0 - lam_init)

        s = jnp.where(causal, _dot_nt(qm_ref[...], kpad[:, HALF:]), NEG)
        m_own = jnp.max(s, axis=1, keepdims=True)
        p_own = jnp.exp(s - m_own)
        l_own = jnp.sum(p_own, axis=1, keepdims=True)
        acc_own = _dot(p_own.astype(BF16), vpad[:, HALF:])
        gate, mrow = gate_ref[...], mrow_ref[...]
        sel = _rank_select(gate, n_blocks, _iota(gate.shape, 1) < n_blocks)
        m_all = jnp.maximum(m_own, jnp.max(jnp.where(sel, mrow, NEG), axis=1, keepdims=True))
        w_own = jnp.exp(m_own - m_all)
        wrow = jnp.where(sel, jnp.exp(mrow - m_all), 0.0)
        num = w_own * acc_own
        den = w_own * l_own + jnp.sum(wrow * lrow_ref[...], axis=1, keepdims=True)
        for j in range(n_blocks):
            num = num + wrow[:, j:j + 1] * accb_ref[j]
        ob = jnp.where(_head_mask(num.shape, t_new, HEAD_DIM), num / den, 0.0)
        o_ref[0, :, HALF:] = _fold_heads(ob, t_new)


def _odd_decode_kernel(pt_ref, q_ref, kn_ref, vn_ref, *rest, t_new):
    del pt_ref
    ck_refs, cv_refs = rest[:DEC_PAGES], rest[DEC_PAGES:2 * DEC_PAGES]
    o_ref, qs_ref, carry_ref, acc_ref = rest[2 * DEC_PAGES:]
    p = pl.program_id(1)
    n_heads = D_MODEL // HEAD_DIM
    rows = n_heads * t_new

    def page(k_bf, v_bf, strict, w):
        z = _dot_nt(qs_ref[...], k_bf)
        sp = _softplus(z)
        if strict is not None:
            sp = jnp.where(strict, sp, 0.0)
        suffix, rowsum = _sb_suffix(sp, w)
        return z, suffix, rowsum

    @pl.when(p == 0)
    def _():
        qs_ref[...] = (_block_diag(q_ref[0], n_heads, HEAD_DIM) * SCALE).astype(BF16)
        acc_ref[...] = jnp.zeros(acc_ref.shape, F32)
        kpad = _pad_rows(kn_ref[0], PAGE_SIZE).astype(BF16)
        vpad = _pad_rows(vn_ref[0], PAGE_SIZE).astype(BF16)
        shape = (rows, PAGE_SIZE)
        strict = _iota(shape, 1) < (_iota(shape, 0) & (t_new - 1))
        z, suffix, rowsum = page(kpad, vpad, strict, _sb_weights())
        a = jnp.where(strict, jnp.exp(z - suffix), 0.0)
        acc_ref[...] += _dot(a.astype(BF16), vpad)
        carry_ref[...] = rowsum

    def pages():
        w = _sb_weights()
        carry = carry_ref[...]
        weights = []
        for u in range(DEC_PAGES):
            z, suffix, rowsum = page(ck_refs[u][...].astype(BF16), None, None, w)
            weights.append(jnp.exp(z - suffix - carry).astype(BF16))
            carry = carry + rowsum
        carry_ref[...] = carry
        vcat = jnp.concatenate([r[...] for r in cv_refs], axis=0).astype(BF16)
        acc_ref[...] += _dot(jnp.concatenate(weights, axis=1), vcat)

    pages()

    @pl.when(p == N_PAGES // DEC_PAGES - 1)
    def _():
        acc = acc_ref[...]
        o_ref[0] = _fold_heads(jnp.where(_head_mask(acc.shape, t_new, HEAD_DIM), acc, 0.0), t_new)


def _decode_attention(kernel, q, k_new, v_new, cache_k, cache_v, page_table, layer, page_of, extra_in, scratch):
    b, t_new, _ = q.shape
    new = pl.BlockSpec((1, t_new, D_MODEL), lambda bi, p, pt: (bi, 0, 0))

    def page(u):
        return pl.BlockSpec((None, None, PAGE_SIZE, D_MODEL),
                            lambda bi, p, pt: (layer, pt[bi, page_of(p, u)], 0, 0))

    pages = [page(u) for u in range(DEC_PAGES)]
    extra_specs = [pl.BlockSpec(a.shape, lambda bi, p, pt: (0, 0)) for a in extra_in]
    return pl.pallas_call(
        kernel,
        grid_spec=pltpu.PrefetchScalarGridSpec(
            num_scalar_prefetch=1,
            grid=(b, N_PAGES // DEC_PAGES),
            in_specs=extra_specs + [new, new, new] + pages + pages,
            out_specs=new,
            scratch_shapes=scratch,
        ),
        out_shape=jax.ShapeDtypeStruct((b, t_new, D_MODEL), F32),
        compiler_params=_params(("parallel", "arbitrary")),
    )(page_table, *extra_in, q, k_new, v_new, *([cache_k] * DEC_PAGES), *([cache_v] * DEC_PAGES))


def _decode_even_attention(q, k_new, v_new, cache_k, cache_v, page_table, layer, lam_vecs, subln, lam_init):
    t_new = q.shape[1]
    rows = (HALF // HEAD_DIM) * t_new
    n_blocks = PAST_LEN // MOBA_BLOCK
    n_steps = N_PAGES // DEC_PAGES
    scratch = [
        pltpu.VMEM((rows, HALF), BF16), pltpu.VMEM((rows, HALF), BF16), pltpu.VMEM((rows, HALF), F32),
        pltpu.VMEM((rows, 1), F32), pltpu.VMEM((rows, 1), F32), pltpu.VMEM((rows, HALF), F32),
        pltpu.VMEM((n_blocks, rows, HALF), F32),
        pltpu.VMEM((rows, LANES), F32), pltpu.VMEM((rows, LANES), F32), pltpu.VMEM((rows, LANES), F32),
    ]
    page_of = lambda p, u: p * DEC_PAGES + u
    return _decode_attention(
        functools.partial(_even_decode_kernel, lam_init=lam_init, t_new=t_new),
        q, k_new, v_new, cache_k, cache_v, page_table, layer, page_of, (lam_vecs, subln), scratch)


def _decode_odd_attention(q, k_new, v_new, cache_k, cache_v, page_table, layer):
    t_new = q.shape[1]
    rows = (D_MODEL // HEAD_DIM) * t_new
    scratch = [pltpu.VMEM((rows, D_MODEL), BF16), pltpu.VMEM((rows, LANES), F32), pltpu.VMEM((rows, D_MODEL), F32)]
    page_of = lambda p, u: N_PAGES - p * DEC_PAGES - 1 - u
    return _decode_attention(
        functools.partial(_odd_decode_kernel, t_new=t_new),
        q, k_new, v_new, cache_k, cache_v, page_table, layer, page_of, (), scratch)


def _rope_tables(pos):
    half = HEAD_DIM // 2
    inv_freq = 1.0 / (ROPE_THETA ** (jnp.arange(half, dtype=F32) * (2.0 / HEAD_DIM)))
    ang = pos.astype(F32)[:, None] * inv_freq[None, :]
    cos, sin, zero = jnp.cos(ang), jnp.sin(ang), jnp.zeros_like(ang)
    reps = LANES // HEAD_DIM
    return (jnp.concatenate([cos, cos] * reps, axis=1),
            jnp.concatenate([-sin, zero] * reps, axis=1),
            jnp.concatenate([zero, sin] * reps, axis=1))


def _lambda_init(layer):
    return 0.8 - 0.6 * math.exp(-0.3 * layer)


def _trunk(x3, tables, cache, weights):
    b, t, _ = x3.shape
    n = b * t
    x = x3.reshape(n, D_MODEL)
    gains = weights["norm_gains"]
    new_k, new_v = [], []
    for l in range(DEPTH):
        g = [gains[l, i].reshape(1, D_MODEL) for i in range(6)]
        x = _ffn(x, g[0], g[1], weights["ffn1_w_in"], weights["ffn1_w_out"], l)
        even = l % 2 == 0
        q, k, v = _inproj(x, g[2], weights["mix_w_in"], l, tables, even)
        q3, k3, v3 = (a.reshape(b, t, D_MODEL) for a in (q, k, v))
        if even:
            lam_vecs = weights["diff_lambda"][l // 2]
            subln = weights["diff_subln"][l // 2].reshape(1, 2 * HEAD_DIM)
            if cache is None:
                o = _prompt_even_attention(q3, k3, v3, lam_vecs, subln, _lambda_init(l))
            else:
                o = _decode_even_attention(q3, k3, v3, *cache, l, lam_vecs, subln, _lambda_init(l))
        else:
            if cache is None:
                o = _prompt_odd_attention(q3, k3, v3)
            else:
                o = _decode_odd_attention(q3, k3, v3, *cache, l)
        x = _outproj(o.reshape(n, D_MODEL).astype(BF16), x, weights["mix_w_out"], g[3], l)
        x = _ffn(x, g[4], g[5], weights["ffn2_w_in"], weights["ffn2_w_out"], l)
        new_k.append(k3)
        new_v.append(v3)
    return x.reshape(b, t, D_MODEL), jnp.stack(new_k), jnp.stack(new_v)


def _even_layer_in_proj(w):
    return jnp.concatenate([w[:, c * HALF:(c + 1) * HALF] for c in (0, 3, 1, 4, 2, 5)], axis=1)


def kernel(x_prompt, x_sample, cache_k, cache_v, page_table, norm_gains, ffn1_w_in, ffn1_w_out, mix_w_in,
           mix_w_out, diff_lambda, diff_subln, ffn2_w_in, ffn2_w_out):
    mix_in = jnp.stack([_even_layer_in_proj(mix_w_in[l]) if l % 2 == 0 else mix_w_in[l] for l in range(DEPTH)])
    weights = {
        "norm_gains": norm_gains,
        "ffn1_w_in": ffn1_w_in.astype(BF16), "ffn1_w_out": ffn1_w_out.astype(BF16),
        "ffn2_w_in": ffn2_w_in.astype(BF16), "ffn2_w_out": ffn2_w_out.astype(BF16),
        "mix_w_in": mix_in.astype(BF16), "mix_w_out": mix_w_out.astype(BF16),
        "diff_lambda": diff_lambda, "diff_subln": diff_subln,
    }
    seq = x_prompt.shape[1]
    t_new = x_sample.shape[1]
    prompt_tables = _rope_tables(jnp.arange(seq, dtype=jnp.int32))
    sample_pos = PAST_LEN + (jnp.arange(FFN_TM, dtype=jnp.int32) % t_new)
    sample_tables = _rope_tables(sample_pos)

    y_prompt, k_prompt, v_prompt = _trunk(x_prompt, prompt_tables, None, weights)
    y_sample, k_sample, v_sample = _trunk(x_sample, sample_tables, (cache_k, cache_v, page_table), weights)
    return (y_prompt, y_sample, k_prompt, v_prompt, k_sample, v_sample)
```

```python
import functools
import math

import jax
import jax.numpy as jnp
from jax import lax
from jax.experimental import pallas as pl
from jax.experimental.pallas import tpu as pltpu

F32 = jnp.float32
BF16 = jnp.bfloat16

D_MODEL = 1024
HEAD_DIM = 64
D_FF = 2816
DEPTH = 4
PAGE_SIZE = 128
PAST_LEN = 2048
N_PAGES = PAST_LEN // PAGE_SIZE
MOBA_BLOCK = 256
MOBA_TOPK = 3
ROPE_THETA = 10000.0
NORM_EPS = 1e-6
SCALE = HEAD_DIM ** -0.5
HALF = D_MODEL // 2
LANES = 128
NEG = -1e30

VMEM_LIMIT = 56 * 1024 * 1024

FFN_TM = 512
FFN_TF = 1408
ATT_KB = MOBA_BLOCK
ATT_TQ = 2 * ATT_KB
SB_TK = 128
SB_KB = 256
SB_TQ = 512
DEC_PAGES = 8


def _params(sem):
    return pltpu.CompilerParams(dimension_semantics=sem, vmem_limit_bytes=VMEM_LIMIT)


def _rms(x, g):
    return x * lax.rsqrt(jnp.mean(x * x, axis=-1, keepdims=True) + NORM_EPS) * g


def _dot_nt(a, b):
    return lax.dot_general(a, b, (((1,), (1,)), ((), ())), preferred_element_type=F32)


def _dot(a, b):
    return jnp.dot(a, b, preferred_element_type=F32)


def _iota(shape, dim):
    return lax.broadcasted_iota(jnp.int32, shape, dim)


def _ffn_kernel(x_ref, gpre_ref, gpost_ref, wg_ref, wu_ref, wo_ref, o_ref, xn_ref, acc_ref):
    j = pl.program_id(1)

    @pl.when(j == 0)
    def _():
        xn_ref[...] = _rms(x_ref[...], gpre_ref[...]).astype(BF16)

    xn = xn_ref[...]
    gate = _dot(xn, wg_ref[...])
    up = _dot(xn, wu_ref[...])
    h = (gate * jax.nn.sigmoid(gate) * up).astype(BF16)
    part = _dot(h, wo_ref[...])

    @pl.when(j == 0)
    def _():
        acc_ref[...] = part

    @pl.when(j > 0)
    def _():
        acc_ref[...] += part

    @pl.when(j == pl.num_programs(1) - 1)
    def _():
        o_ref[...] = x_ref[...] + 0.5 * _rms(acc_ref[...], gpost_ref[...])


def _ffn(x, g_pre, g_post, w_in, w_out, layer):
    n = x.shape[0]
    nf = D_FF // FFN_TF
    return pl.pallas_call(
        _ffn_kernel,
        grid=(n // FFN_TM, nf),
        in_specs=[
            pl.BlockSpec((FFN_TM, D_MODEL), lambda i, j: (i, 0)),
            pl.BlockSpec((1, D_MODEL), lambda i, j: (0, 0)),
            pl.BlockSpec((1, D_MODEL), lambda i, j: (0, 0)),
            pl.BlockSpec((None, D_MODEL, FFN_TF), lambda i, j: (layer, 0, j)),
            pl.BlockSpec((None, D_MODEL, FFN_TF), lambda i, j: (layer, 0, j + nf)),
            pl.BlockSpec((None, FFN_TF, D_MODEL), lambda i, j: (layer, j, 0)),
        ],
        out_specs=pl.BlockSpec((FFN_TM, D_MODEL), lambda i, j: (i, 0)),
        out_shape=jax.ShapeDtypeStruct((n, D_MODEL), F32),
        scratch_shapes=[pltpu.VMEM((FFN_TM, D_MODEL), BF16), pltpu.VMEM((FFN_TM, D_MODEL), F32)],
        compiler_params=_params(("parallel", "arbitrary")),
    )(x, g_pre, g_post, w_in, w_in, w_out)


def _inproj_kernel(x_ref, g_ref, w_ref, cos_ref, sina_ref, sinb_ref, *rest, rotary):
    xn = _rms(x_ref[...], g_ref[...]).astype(BF16)
    outs = rest[-3:]
    for c in range(3):
        y = _dot(xn, w_ref[:, c * D_MODEL:(c + 1) * D_MODEL])
        if rotary and c < 2:
            cos, sina, sinb = cos_ref[...], sina_ref[...], sinb_ref[...]
            for p in range(D_MODEL // LANES):
                yp = y[:, p * LANES:(p + 1) * LANES]
                fwd = pltpu.roll(yp, LANES - HEAD_DIM // 2, 1)
                bwd = pltpu.roll(yp, HEAD_DIM // 2, 1)
                outs[c][:, p * LANES:(p + 1) * LANES] = yp * cos + fwd * sina + bwd * sinb
        else:
            outs[c][...] = y


def _inproj(x, g, w, layer, tables, rotary, k_all, v_all):
    n = x.shape[0]
    cos, sina, sinb = tables
    nt = cos.shape[0] // FFN_TM
    tab = pl.BlockSpec((FFN_TM, LANES), lambda i: (i % nt, 0))
    row = pl.BlockSpec((FFN_TM, D_MODEL), lambda i: (i, 0))
    layer_rows = pl.BlockSpec((None, FFN_TM, D_MODEL), lambda i: (layer, i, 0))
    return pl.pallas_call(
        functools.partial(_inproj_kernel, rotary=rotary),
        grid=(n // FFN_TM,),
        in_specs=[
            row,
            pl.BlockSpec((1, D_MODEL), lambda i: (0, 0)),
            pl.BlockSpec((None, D_MODEL, 3 * D_MODEL), lambda i: (layer, 0, 0)),
            tab, tab, tab,
            pl.BlockSpec(memory_space=pl.ANY), pl.BlockSpec(memory_space=pl.ANY),
        ],
        out_specs=[row, layer_rows, layer_rows],
        out_shape=[jax.ShapeDtypeStruct((n, D_MODEL), F32),
                   jax.ShapeDtypeStruct(k_all.shape, F32), jax.ShapeDtypeStruct(v_all.shape, F32)],
        input_output_aliases={6: 1, 7: 2},
        compiler_params=_params(("parallel",)),
    )(x, g, w, cos, sina, sinb, k_all, v_all)


def _outproj_kernel(o_ref, x_ref, w_ref, g_ref, y_ref):
    m = _dot(o_ref[...], w_ref[...])
    y_ref[...] = x_ref[...] + _rms(m, g_ref[...])


def _outproj(o, x, w, g, layer):
    n = x.shape[0]
    row = pl.BlockSpec((FFN_TM, D_MODEL), lambda i: (i, 0))
    return pl.pallas_call(
        _outproj_kernel,
        grid=(n // FFN_TM,),
        in_specs=[
            row, row,
            pl.BlockSpec((None, D_MODEL, D_MODEL), lambda i: (layer, 0, 0)),
            pl.BlockSpec((1, D_MODEL), lambda i: (0, 0)),
        ],
        out_specs=row,
        out_shape=jax.ShapeDtypeStruct((n, D_MODEL), F32),
        compiler_params=_params(("parallel",)),
    )(o, x, w, g)


def _stack_heads(x):
    low = _iota(x.shape, 1) < HEAD_DIM
    return jnp.concatenate([jnp.where(low, x, 0.0), jnp.where(low, 0.0, x)], axis=0)


def _softmax_step(s, vb, m_ref, l_ref, acc_ref):
    m_old = m_ref[...]
    m_new = jnp.maximum(m_old, jnp.max(s, axis=1, keepdims=True))
    alpha = jnp.exp(m_old - m_new)
    p = jnp.exp(s - m_new)
    l_ref[...] = alpha * l_ref[...] + jnp.sum(p, axis=1, keepdims=True)
    acc_ref[...] = alpha * acc_ref[...] + _dot(p.astype(BF16), vb)
    m_ref[...] = m_new


def _softmax_step_t(s, m_ref, l_ref):
    m_old = m_ref[...]
    m_new = jnp.maximum(m_old, jnp.max(s, axis=0, keepdims=True))
    alpha = jnp.exp(m_old - m_new)
    p = jnp.exp(s - m_new)
    l_ref[...] = alpha * l_ref[...] + jnp.sum(p, axis=0, keepdims=True)
    m_ref[...] = m_new
    return alpha, p.astype(BF16)


def _lambda_value(lam_ref, lam_init):
    lv = lam_ref[...]
    a = jnp.sum(lv[0:1] * lv[1:2], axis=1, keepdims=True)
    b = jnp.sum(lv[2:3] * lv[3:4], axis=1, keepdims=True)
    return jnp.exp(a) - jnp.exp(b) + lam_init


def _sb_weights():
    shape = (2 * SB_TK, 2 * SB_TK)
    j = _iota(shape, 0) & (SB_TK - 1)
    s = _iota(shape, 1)
    return jnp.where((s >= SB_TK) | (j >= s), 1.0, 0.0).astype(BF16)


def _sb_suffix(sp, w):
    hi = sp.astype(BF16)
    lo = (sp - hi.astype(F32)).astype(BF16)
    r = _dot(jnp.concatenate([hi, lo], axis=1), w)
    return r[:, :SB_TK], r[:, SB_TK:]


def _softplus(z):
    return jnp.maximum(z, 0.0) + jnp.log(1.0 + jnp.exp(-jnp.abs(z)))


def _rank_select_t(gate, eligible):
    blk = _iota(gate.shape, 0)
    g = jnp.where(eligible, gate, -jnp.inf)
    cnt = jnp.zeros(gate.shape, jnp.int32)
    for jp in range(gate.shape[0]):
        row = g[jp:jp + 1, :]
        beats = (row > g) | ((row == g) & (jp < blk))
        cnt = cnt + jnp.where(beats, 1, 0)
    return (cnt < MOBA_TOPK) & eligible


def _rank_select(gate, n_blocks, eligible):
    lane = _iota(gate.shape, 1)
    g = jnp.where(eligible, gate, -jnp.inf)
    cnt = jnp.zeros(gate.shape, jnp.int32)
    for jp in range(n_blocks):
        col = g[:, jp:jp + 1]
        beats = (col > g) | ((col == g) & (jp < lane))
        cnt = cnt + jnp.where(beats, 1, 0)
    return (cnt < MOBA_TOPK) & eligible


def _store_v_transposed(v_ref, vt_ref):
    tk = vt_ref.shape[2]
    for c in range(vt_ref.shape[0]):
        vt_ref[c] = v_ref[0, c * tk:(c + 1) * tk, :].T.astype(BF16)


def _own_causal_t(shape, u):
    return (u * ATT_KB + (_iota(shape, 0) & (ATT_KB - 1))) <= _iota(shape, 1)


def _pipelined_blocks(i, scores, consume, st_ref):
    st_ref[...] = scores(2 * i + 1, 1)
    own = scores(2 * i, 0)
    consume(st_ref[...], 2 * i + 1)
    st_ref[...] = own

    def pair(n, c):
        j = 2 * n
        first = scores(j, None)
        consume(st_ref[...], jnp.where(j == 0, 2 * i, j - 1))
        st_ref[...] = scores(j + 1, None)
        consume(first, j)
        return c

    lax.fori_loop(0, i, pair, 0)
    consume(st_ref[...], jnp.maximum(2 * i - 1, 0))


def _diff_prompt_kernel(lam_ref, sg_ref, q_ref, k_ref, v_ref, o_ref, vt_ref, st_ref, m_ref, l_ref, acc_ref, *,
                        lam_init):
    i = pl.program_id(2)
    tk = ATT_KB

    @pl.when(i == 0)
    def _():
        _store_v_transposed(v_ref, vt_ref)

    qs = (q_ref[0] * SCALE).astype(BF16)
    m_ref[...] = jnp.full(m_ref.shape, NEG, F32)
    l_ref[...] = jnp.zeros(l_ref.shape, F32)
    acc_ref[...] = jnp.zeros(acc_ref.shape, F32)

    def scores(j, own):
        kb = k_ref[0, pl.ds(pl.multiple_of(j * tk, tk), tk), :].astype(BF16)
        st = _dot_nt(_stack_heads(kb), qs)
        return st if own is None else jnp.where(_own_causal_t(st.shape, own), st, NEG)

    def consume(st, j):
        vt = vt_ref[j]
        for mp in range(2):
            alpha, p = _softmax_step_t(st[mp * tk:(mp + 1) * tk], m_ref.at[mp], l_ref.at[mp])
            acc_ref[mp] = alpha * acc_ref[mp] + _dot(vt, p)

    _pipelined_blocks(i, scores, consume, st_ref)

    a = acc_ref[0] / l_ref[0] - _lambda_value(lam_ref, lam_init) * (acc_ref[1] / l_ref[1])
    y = a * lax.rsqrt(jnp.mean(a * a, axis=0, keepdims=True) + NORM_EPS) * sg_ref[...] * (1.0 - lam_init)
    o_ref[0] = y.T.astype(o_ref.dtype)


def _moba_prompt_kernel(q_ref, k_ref, v_ref, o_ref, vt_ref, st_ref, kbar_ref, bias_ref, m_ref, l_ref, acc_ref):
    i = pl.program_id(2)
    tq, tk = ATT_TQ, ATT_KB
    n_blocks = k_ref.shape[1] // tk

    @pl.when(i == 0)
    def _():
        _store_v_transposed(v_ref, vt_ref)
        low = _iota((1, LANES), 1) < HEAD_DIM
        for jb in range(n_blocks):
            blk = k_ref[0, jb * tk:(jb + 1) * tk, :]
            kbar = jnp.sum(blk, axis=0, keepdims=True) * (1.0 / tk)
            kbar_ref[0, jb:jb + 1, :] = jnp.where(low, kbar, 0.0)
            kbar_ref[1, jb:jb + 1, :] = jnp.where(low, 0.0, kbar)

    q = q_ref[0]
    own_block = _iota((n_blocks, tq), 1) >> (tk.bit_length() - 1)
    for h in range(2):
        gate = lax.dot_general(kbar_ref[h], q, (((1,), (1,)), ((), ())),
                               precision=lax.Precision.HIGHEST, preferred_element_type=F32)
        sel = _rank_select_t(gate, _iota(gate.shape, 0) < 2 * i + own_block)
        bias_ref[h] = jnp.where(sel, 0.0, NEG)
    qs = (q * SCALE).astype(BF16)

    m_ref[...] = jnp.full(m_ref.shape, NEG, F32)
    l_ref[...] = jnp.zeros(l_ref.shape, F32)
    acc_ref[...] = jnp.zeros(acc_ref.shape, F32)
    head_a = _iota((LANES, tq), 0) < HEAD_DIM

    def scores(j, own):
        kb = k_ref[0, pl.ds(pl.multiple_of(j * tk, tk), tk), :].astype(BF16)
        st = _dot_nt(_stack_heads(kb), qs)
        biased = jnp.concatenate([st[h * tk:(h + 1) * tk] + bias_ref[h, pl.ds(j, 1), :] for h in range(2)], axis=0)
        if own is None:
            return biased
        inside = (_iota(st.shape, 1) >> (tk.bit_length() - 1)) == own
        return jnp.where(inside, jnp.where(_own_causal_t(st.shape, own), st, NEG), biased)

    def consume(st, j):
        alphas, ps = [], []
        for h in range(2):
            alpha, p = _softmax_step_t(st[h * tk:(h + 1) * tk], m_ref.at[h], l_ref.at[h])
            alphas.append(alpha)
            ps.append(p)
        vt = vt_ref[j]
        keep = _iota(vt.shape, 0) < HEAD_DIM
        vbd = jnp.concatenate([jnp.where(keep, vt, 0.0), jnp.where(keep, 0.0, vt)], axis=1)
        alpha = jnp.where(head_a, alphas[0], alphas[1])
        acc_ref[...] = alpha * acc_ref[...] + _dot(vbd, jnp.concatenate(ps, axis=0))

    _pipelined_blocks(i, scores, consume, st_ref)

    o = acc_ref[...] / jnp.where(head_a, l_ref[0], l_ref[1])
    o_ref[0] = o.T.astype(o_ref.dtype)


def _sb_scores(qs, kb, w, strict):
    z = _dot_nt(qs, _stack_heads(kb))
    if strict is not None:
        z = jnp.where(strict, z, NEG)
    return _sb_reduce(z, w)


def _sb_reduce(z, w):
    sp = _softplus(z)
    parts = [_sb_suffix(sp[:, g * SB_TK:(g + 1) * SB_TK], w) for g in range(z.shape[1] // SB_TK)]
    suffix = jnp.concatenate([p[0] for p in parts], axis=1)
    rowsum = jnp.concatenate([p[1] for p in parts], axis=1)
    return z - suffix, rowsum


def _sb_apply(t, rowsum, vb, carry_ref, acc_ref):
    groups = t.shape[1] // (2 * SB_TK)
    carry = carry_ref[...]
    weights = [None] * (2 * groups)
    carries = []
    for h in range(2):
        c = carry[:, h * SB_TK:(h + 1) * SB_TK]
        for u in reversed(range(groups)):
            g = h * groups + u
            weights[g] = jnp.exp(t[:, g * SB_TK:(g + 1) * SB_TK] - c).astype(BF16)
            c = c + rowsum[:, g * SB_TK:(g + 1) * SB_TK]
        carries.append(c)
    acc_ref[...] += _dot(jnp.concatenate(weights, axis=1), _stack_heads(vb))
    carry_ref[...] = jnp.concatenate(carries, axis=1)


def _sb_prompt_kernel(q_ref, k_ref, v_ref, o_ref, t_ref, rs_ref, carry_ref, acc_ref):
    i = pl.program_id(2)
    tq, tk = SB_TQ, SB_KB
    sub = tq // tk
    qs = (q_ref[0] * SCALE).astype(BF16)
    w = _sb_weights()
    carry_ref[...] = jnp.zeros(carry_ref.shape, F32)
    acc_ref[...] = jnp.zeros(acc_ref.shape, F32)

    def apply(t, rowsum, j):
        vb = v_ref[0, pl.ds(pl.multiple_of(j * tk, tk), tk), :].astype(BF16)
        _sb_apply(t, rowsum, vb, carry_ref, acc_ref)

    pending = None
    for u in reversed(range(sub)):
        r0 = u * tk
        shape = (tq - r0, 2 * tk)
        strict = (_iota(shape, 1) & (tk - 1)) < _iota(shape, 0)
        kb = k_ref[0, pl.ds(pl.multiple_of((i * sub + u) * tk, tk), tk), :].astype(BF16)
        fresh = _sb_scores(qs[r0:], kb, w, strict)
        if pending is not None:
            rows = pl.ds(r0 + tk, tq - r0 - tk)
            vb = v_ref[0, pl.ds(pl.multiple_of((i * sub + u + 1) * tk, tk), tk), :].astype(BF16)
            _sb_apply(*pending, vb, carry_ref.at[rows], acc_ref.at[rows])
        pending = fresh
    t_ref[...], rs_ref[...] = pending

    def raw(j):
        kb = k_ref[0, pl.ds(pl.multiple_of(j * tk, tk), tk), :].astype(BF16)
        return _dot_nt(qs, _stack_heads(kb))

    def body(n, c):
        j = (i - n) * sub - 1
        z = raw(j)
        apply(t_ref[...], rs_ref[...], j + 1)
        for u in range(1, sub):
            t, rowsum = _sb_reduce(z, w)
            z = raw(j - u)
            apply(t, rowsum, j - u + 1)
        t_ref[...], rs_ref[...] = _sb_reduce(z, w)
        return c

    lax.fori_loop(0, i, body, 0)
    apply(t_ref[...], rs_ref[...], 0)
    o_ref[0] = acc_ref[...].astype(o_ref.dtype)


def _prompt_attention_call(kernel, q, k, v, layer, col0, n_col, extra_in, extra_specs, scratch, tq=ATT_TQ):
    b, t, _ = q.shape
    qspec = pl.BlockSpec((1, tq, LANES), lambda bi, c, i: (bi, i, col0 + c))
    kvspec = pl.BlockSpec((None, 1, t, LANES), lambda bi, c, i: (layer, bi, 0, col0 + c))
    return pl.pallas_call(
        kernel,
        grid=(b, n_col, t // tq),
        in_specs=list(extra_specs) + [qspec, kvspec, kvspec],
        out_specs=pl.BlockSpec((1, tq, LANES), lambda bi, c, i: (bi, i, c)),
        out_shape=jax.ShapeDtypeStruct((b, t, n_col * LANES), BF16),
        scratch_shapes=scratch,
        compiler_params=_params(("parallel", "parallel", "arbitrary")),
    )(*extra_in, q, k, v)


def _prompt_even_attention(q, k, v, layer, lam_vecs, subln, lam_init):
    t = q.shape[1]
    tq = ATT_TQ
    whole = lambda shape: pl.BlockSpec(shape, lambda bi, c, i: (0, 0))
    vt = [pltpu.VMEM((t // ATT_KB, LANES, ATT_KB), BF16),
          pltpu.VMEM((2 * ATT_KB, tq), F32)]
    stats = [pltpu.VMEM((2, 1, tq), F32), pltpu.VMEM((2, 1, tq), F32)]
    sg = jnp.broadcast_to(subln.reshape(2 * HEAD_DIM, 1), (2 * HEAD_DIM, tq))
    o_a = _prompt_attention_call(
        functools.partial(_diff_prompt_kernel, lam_init=lam_init), q, k, v, layer, 0, HALF // LANES,
        (lam_vecs, sg), (whole(lam_vecs.shape), whole(sg.shape)),
        vt + stats + [pltpu.VMEM((2, LANES, tq), F32)])
    n_blocks = t // MOBA_BLOCK
    o_b = _prompt_attention_call(
        _moba_prompt_kernel, q, k, v, layer, HALF // LANES, HALF // LANES, (), (),
        vt + [pltpu.VMEM((2, n_blocks, LANES), F32), pltpu.VMEM((2, n_blocks, tq), F32)] + stats
        + [pltpu.VMEM((LANES, tq), F32)])
    return jnp.concatenate([o_a, o_b], axis=-1)


def _prompt_odd_attention(q, k, v, layer):
    return _prompt_attention_call(
        _sb_prompt_kernel, q, k, v, layer, 0, D_MODEL // LANES, (), (),
        [pltpu.VMEM((SB_TQ, 2 * SB_KB), F32)] * 2 + [pltpu.VMEM((SB_TQ, 2 * SB_TK), F32),
                                                     pltpu.VMEM((SB_TQ, LANES), F32)], tq=SB_TQ)


def _block_diag(q, n_heads, width):
    t = q.shape[0]
    tiled = jnp.concatenate([q] * n_heads, axis=0)
    keep = _head_mask(tiled.shape, t, width)
    return jnp.where(keep, tiled, 0.0)


def _head_mask(shape, rows_per_head, width):
    shift_r = rows_per_head.bit_length() - 1
    shift_c = width.bit_length() - 1
    return (_iota(shape, 0) >> shift_r) == (_iota(shape, 1) >> shift_c)


def _fold_heads(x, t):
    out = x[0:t]
    for h in range(1, x.shape[0] // t):
        out = out + x[h * t:(h + 1) * t]
    return out


def _pad_rows(x, rows):
    return jnp.concatenate([x, jnp.zeros((rows - x.shape[0], x.shape[1]), x.dtype)], axis=0)


def _even_decode_kernel(pt_ref, lam_ref, sg_ref, q_ref, kn_ref, vn_ref, *rest, lam_init, t_new):
    del pt_ref
    ck_refs, cv_refs = rest[:DEC_PAGES], rest[DEC_PAGES:2 * DEC_PAGES]
    (o_ref, qd_ref, qm_ref, qmf_ref, md_ref, ld_ref, accd_ref,
     accb_ref, gate_ref, mrow_ref, lrow_ref) = rest[2 * DEC_PAGES:]
    p = pl.program_id(1)
    n_steps = N_PAGES // DEC_PAGES
    n_maps = HALF // HEAD_DIM
    rows = n_maps * t_new
    blocks_per_step = DEC_PAGES * PAGE_SIZE // MOBA_BLOCK
    n_blocks = PAST_LEN // MOBA_BLOCK

    @pl.when(p == 0)
    def _():
        q = q_ref[0]
        qd_ref[...] = (_block_diag(q[:, :HALF], n_maps, HEAD_DIM) * SCALE).astype(BF16)
        qmf = _block_diag(q[:, HALF:], n_maps, HEAD_DIM)
        qmf_ref[...] = qmf
        qm_ref[...] = (qmf * SCALE).astype(BF16)
        md_ref[...] = jnp.full(md_ref.shape, NEG, F32)
        ld_ref[...] = jnp.zeros(ld_ref.shape, F32)
        accd_ref[...] = jnp.zeros(accd_ref.shape, F32)
        gate_ref[...] = jnp.zeros(gate_ref.shape, F32)
        mrow_ref[...] = jnp.zeros(mrow_ref.shape, F32)
        lrow_ref[...] = jnp.zeros(lrow_ref.shape, F32)

    def pages():
        kcat = jnp.concatenate([r[...] for r in ck_refs], axis=0)
        vcat = jnp.concatenate([r[...] for r in cv_refs], axis=0)
        s = _dot_nt(qd_ref[...], kcat[:, :HALF].astype(BF16))
        _softmax_step(s, vcat[:, :HALF].astype(BF16), md_ref, ld_ref, accd_ref)
        km = kcat[:, HALF:]
        vm = vcat[:, HALF:].astype(BF16)
        s = _dot_nt(qm_ref[...], km.astype(BF16))
        gate, mrow, lrow = gate_ref[...], mrow_ref[...], lrow_ref[...]
        for u in range(blocks_per_step):
            jb = p * blocks_per_step + u
            sb = s[:, u * MOBA_BLOCK:(u + 1) * MOBA_BLOCK]
            m = jnp.max(sb, axis=1, keepdims=True)
            pe = jnp.exp(sb - m)
            accb_ref[jb] = _dot(pe.astype(BF16), vm[u * MOBA_BLOCK:(u + 1) * MOBA_BLOCK])
            kbar = jnp.sum(km[u * MOBA_BLOCK:(u + 1) * MOBA_BLOCK], axis=0, keepdims=True) * (1.0 / MOBA_BLOCK)
            g = jnp.sum(qmf_ref[...] * kbar, axis=1, keepdims=True)
            here = _iota(gate.shape, 1) == jb
            gate = jnp.where(here, g, gate)
            mrow = jnp.where(here, m, mrow)
            lrow = jnp.where(here, jnp.sum(pe, axis=1, keepdims=True), lrow)
        gate_ref[...], mrow_ref[...], lrow_ref[...] = gate, mrow, lrow

    pages()

    @pl.when(p == n_steps - 1)
    def _():
        kpad = _pad_rows(kn_ref[0], PAGE_SIZE).astype(BF16)
        vpad = _pad_rows(vn_ref[0], PAGE_SIZE).astype(BF16)
        shape = (rows, PAGE_SIZE)
        causal = _iota(shape, 1) <= (_iota(shape, 0) & (t_new - 1))

        s = jnp.where(causal, _dot_nt(qd_ref[...], kpad[:, :HALF]), NEG)
        _softmax_step(s, vpad[:, :HALF], md_ref, ld_ref, accd_ref)
        od = accd_ref[...] / ld_ref[...]
        map_id = _iota(od.shape, 0) >> (t_new.bit_length() - 1)
        coef = jnp.where((map_id & 1) == 0, 1.0, -_lambda_value(lam_ref, lam_init))
        own = (map_id >> 1) == (_iota(od.shape, 1) >> ((2 * HEAD_DIM).bit_length() - 1))
        oa = _fold_heads(jnp.where(own, od * coef, 0.0), t_new)
        sg = sg_ref[...]
        for h in range(HALF // LANES):
            seg = oa[:, h * LANES:(h + 1) * LANES]
            o_ref[0, :, h * LANES:(h + 1) * LANES] = _rms(seg, sg) * (1.0 - lam_init)

        s = jnp.where(causal, _dot_nt(qm_ref[...], kpad[:, HALF:]), NEG)
        m_own = jnp.max(s, axis=1, keepdims=True)
        p_own = jnp.exp(s - m_own)
        l_own = jnp.sum(p_own, axis=1, keepdims=True)
        acc_own = _dot(p_own.astype(BF16), vpad[:, HALF:])
        gate, mrow = gate_ref[...], mrow_ref[...]
        sel = _rank_select(gate, n_blocks, _iota(gate.shape, 1) < n_blocks)
        m_all = jnp.maximum(m_own, jnp.max(jnp.where(sel, mrow, NEG), axis=1, keepdims=True))
        w_own = jnp.exp(m_own - m_all)
        wrow = jnp.where(sel, jnp.exp(mrow - m_all), 0.0)
        num = w_own * acc_own
        den = w_own * l_own + jnp.sum(wrow * lrow_ref[...], axis=1, keepdims=True)
        for j in range(n_blocks):
            num = num + wrow[:, j:j + 1] * accb_ref[j]
        ob = jnp.where(_head_mask(num.shape, t_new, HEAD_DIM), num / den, 0.0)
        o_ref[0, :, HALF:] = _fold_heads(ob, t_new)


def _odd_decode_kernel(pt_ref, q_ref, kn_ref, vn_ref, *rest, t_new):
    del pt_ref
    ck_refs, cv_refs = rest[:DEC_PAGES], rest[DEC_PAGES:2 * DEC_PAGES]
    o_ref, qs_ref, carry_ref, acc_ref = rest[2 * DEC_PAGES:]
    p = pl.program_id(1)
    n_heads = D_MODEL // HEAD_DIM
    rows = n_heads * t_new

    def page(k_bf, v_bf, strict, w):
        z = _dot_nt(qs_ref[...], k_bf)
        sp = _softplus(z)
        if strict is not None:
            sp = jnp.where(strict, sp, 0.0)
        suffix, rowsum = _sb_suffix(sp, w)
        return z, suffix, rowsum

    @pl.when(p == 0)
    def _():
        qs_ref[...] = (_block_diag(q_ref[0], n_heads, HEAD_DIM) * SCALE).astype(BF16)
        acc_ref[...] = jnp.zeros(acc_ref.shape, F32)
        kpad = _pad_rows(kn_ref[0], PAGE_SIZE).astype(BF16)
        vpad = _pad_rows(vn_ref[0], PAGE_SIZE).astype(BF16)
        shape = (rows, PAGE_SIZE)
        strict = _iota(shape, 1) < (_iota(shape, 0) & (t_new - 1))
        z, suffix, rowsum = page(kpad, vpad, strict, _sb_weights())
        a = jnp.where(strict, jnp.exp(z - suffix), 0.0)
        acc_ref[...] += _dot(a.astype(BF16), vpad)
        carry_ref[...] = rowsum

    def pages():
        w = _sb_weights()
        carry = carry_ref[...]
        weights = []
        for u in range(DEC_PAGES):
            z, suffix, rowsum = page(ck_refs[u][...].astype(BF16), None, None, w)
            weights.append(jnp.exp(z - suffix - carry).astype(BF16))
            carry = carry + rowsum
        carry_ref[...] = carry
        vcat = jnp.concatenate([r[...] for r in cv_refs], axis=0).astype(BF16)
        acc_ref[...] += _dot(jnp.concatenate(weights, axis=1), vcat)

    pages()

    @pl.when(p == N_PAGES // DEC_PAGES - 1)
    def _():
        acc = acc_ref[...]
        o_ref[0] = _fold_heads(jnp.where(_head_mask(acc.shape, t_new, HEAD_DIM), acc, 0.0), t_new)


def _decode_attention(kernel, q, k_new, v_new, cache_k, cache_v, page_table, layer, page_of, extra_in, scratch):
    b, t_new, _ = q.shape
    new = pl.BlockSpec((1, t_new, D_MODEL), lambda bi, p, pt: (bi, 0, 0))
    new_kv = pl.BlockSpec((None, 1, t_new, D_MODEL), lambda bi, p, pt: (layer, bi, 0, 0))

    def page(u):
        return pl.BlockSpec((None, None, PAGE_SIZE, D_MODEL),
                            lambda bi, p, pt: (layer, pt[bi, page_of(p, u)], 0, 0))

    pages = [page(u) for u in range(DEC_PAGES)]
    extra_specs = [pl.BlockSpec(a.shape, lambda bi, p, pt: (0, 0)) for a in extra_in]
    return pl.pallas_call(
        kernel,
        grid_spec=pltpu.PrefetchScalarGridSpec(
            num_scalar_prefetch=1,
            grid=(b, N_PAGES // DEC_PAGES),
            in_specs=extra_specs + [new, new_kv, new_kv] + pages + pages,
            out_specs=new,
            scratch_shapes=scratch,
        ),
        out_shape=jax.ShapeDtypeStruct((b, t_new, D_MODEL), F32),
        compiler_params=_params(("parallel", "arbitrary")),
    )(page_table, *extra_in, q, k_new, v_new, *([cache_k] * DEC_PAGES), *([cache_v] * DEC_PAGES))


def _decode_even_attention(q, k_new, v_new, cache_k, cache_v, page_table, layer, lam_vecs, subln, lam_init):
    t_new = q.shape[1]
    rows = (HALF // HEAD_DIM) * t_new
    n_blocks = PAST_LEN // MOBA_BLOCK
    n_steps = N_PAGES // DEC_PAGES
    scratch = [
        pltpu.VMEM((rows, HALF), BF16), pltpu.VMEM((rows, HALF), BF16), pltpu.VMEM((rows, HALF), F32),
        pltpu.VMEM((rows, 1), F32), pltpu.VMEM((rows, 1), F32), pltpu.VMEM((rows, HALF), F32),
        pltpu.VMEM((n_blocks, rows, HALF), F32),
        pltpu.VMEM((rows, LANES), F32), pltpu.VMEM((rows, LANES), F32), pltpu.VMEM((rows, LANES), F32),
    ]
    page_of = lambda p, u: p * DEC_PAGES + u
    return _decode_attention(
        functools.partial(_even_decode_kernel, lam_init=lam_init, t_new=t_new),
        q, k_new, v_new, cache_k, cache_v, page_table, layer, page_of, (lam_vecs, subln), scratch)


def _decode_odd_attention(q, k_new, v_new, cache_k, cache_v, page_table, layer):
    t_new = q.shape[1]
    rows = (D_MODEL // HEAD_DIM) * t_new
    scratch = [pltpu.VMEM((rows, D_MODEL), BF16), pltpu.VMEM((rows, LANES), F32), pltpu.VMEM((rows, D_MODEL), F32)]
    page_of = lambda p, u: N_PAGES - p * DEC_PAGES - 1 - u
    return _decode_attention(
        functools.partial(_odd_decode_kernel, t_new=t_new),
        q, k_new, v_new, cache_k, cache_v, page_table, layer, page_of, (), scratch)


def _rope_tables(pos):
    half = HEAD_DIM // 2
    inv_freq = 1.0 / (ROPE_THETA ** (jnp.arange(half, dtype=F32) * (2.0 / HEAD_DIM)))
    ang = pos.astype(F32)[:, None] * inv_freq[None, :]
    cos, sin, zero = jnp.cos(ang), jnp.sin(ang), jnp.zeros_like(ang)
    reps = LANES // HEAD_DIM
    return (jnp.concatenate([cos, cos] * reps, axis=1),
            jnp.concatenate([-sin, zero] * reps, axis=1),
            jnp.concatenate([zero, sin] * reps, axis=1))


def _lambda_init(layer):
    return 0.8 - 0.6 * math.exp(-0.3 * layer)


def _trunk(x3, tables, cache, weights):
    b, t, _ = x3.shape
    n = b * t
    x = x3.reshape(n, D_MODEL)
    gains = weights["norm_gains"]
    k_all = jnp.zeros((DEPTH, n, D_MODEL), F32)
    v_all = jnp.zeros((DEPTH, n, D_MODEL), F32)
    for l in range(DEPTH):
        g = [gains[l, i].reshape(1, D_MODEL) for i in range(6)]
        x = _ffn(x, g[0], g[1], weights["ffn1_w_in"], weights["ffn1_w_out"], l)
        even = l % 2 == 0
        q, k_all, v_all = _inproj(x, g[2], weights["mix_w_in"], l, tables, even, k_all, v_all)
        q3 = q.reshape(b, t, D_MODEL)
        k3, v3 = (a.reshape(DEPTH, b, t, D_MODEL) for a in (k_all, v_all))
        if even:
            lam_vecs = weights["diff_lambda"][l // 2]
            subln = weights["diff_subln"][l // 2].reshape(1, 2 * HEAD_DIM)
            if cache is None:
                o = _prompt_even_attention(q3, k3, v3, l, lam_vecs, subln, _lambda_init(l))
            else:
                o = _decode_even_attention(q3, k3, v3, *cache, l, lam_vecs, subln, _lambda_init(l))
        else:
            if cache is None:
                o = _prompt_odd_attention(q3, k3, v3, l)
            else:
                o = _decode_odd_attention(q3, k3, v3, *cache, l)
        x = _outproj(o.reshape(n, D_MODEL).astype(BF16), x, weights["mix_w_out"], g[3], l)
        x = _ffn(x, g[4], g[5], weights["ffn2_w_in"], weights["ffn2_w_out"], l)
    return x.reshape(b, t, D_MODEL), k3, v3


def _even_layer_in_proj(w):
    return jnp.concatenate([w[:, c * HALF:(c + 1) * HALF] for c in (0, 3, 1, 4, 2, 5)], axis=1)


def kernel(x_prompt, x_sample, cache_k, cache_v, page_table, norm_gains, ffn1_w_in, ffn1_w_out, mix_w_in,
           mix_w_out, diff_lambda, diff_subln, ffn2_w_in, ffn2_w_out):
    mix_in = jnp.stack([_even_layer_in_proj(mix_w_in[l]) if l % 2 == 0 else mix_w_in[l] for l in range(DEPTH)])
    weights = {
        "norm_gains": norm_gains,
        "ffn1_w_in": ffn1_w_in.astype(BF16), "ffn1_w_out": ffn1_w_out.astype(BF16),
        "ffn2_w_in": ffn2_w_in.astype(BF16), "ffn2_w_out": ffn2_w_out.astype(BF16),
        "mix_w_in": mix_in.astype(BF16), "mix_w_out": mix_w_out.astype(BF16),
        "diff_lambda": diff_lambda, "diff_subln": diff_subln,
    }
    seq = x_prompt.shape[1]
    t_new = x_sample.shape[1]
    prompt_tables = _rope_tables(jnp.arange(seq, dtype=jnp.int32))
    sample_pos = PAST_LEN + (jnp.arange(FFN_TM, dtype=jnp.int32) % t_new)
    sample_tables = _rope_tables(sample_pos)

    y_prompt, k_prompt, v_prompt = _trunk(x_prompt, prompt_tables, None, weights)
    y_sample, k_sample, v_sample = _trunk(x_sample, sample_tables, (cache_k, cache_v, page_table), weights)
    return (y_prompt, y_sample, k_prompt, v_prompt, k_sample, v_sample)
```
